```python
import jax, jax.numpy as jnp
from jax import lax
import numpy as np

D_MODEL = 1024
BATCH = 32
SEQ = 2048
DEPTH = 1

CHUNK = 64
EPS = 1e-6
D_FF = 4 * D_MODEL
SGU_BLOCK = 128
SGU_GROUPS = 8
SGU_WIDTH = D_MODEL
SGU_GDIM = SGU_WIDTH // SGU_GROUPS
SSM_INNER = 2 * D_MODEL
SSM_HEADDIM = 64
SSM_HEADS = SSM_INNER // SSM_HEADDIM
SSM_GROUPS = 8
SSM_HPG = SSM_HEADS // SSM_GROUPS
SSM_STATE = 128
SSM_CONV = 4
SSM_CONV_DIM = SSM_INNER + 2 * SSM_GROUPS * SSM_STATE
SSM_CHUNK = CHUNK
IN_WIDTHS = (SGU_WIDTH, SGU_WIDTH, SSM_INNER, SSM_CONV_DIM, SSM_HEADS, D_MODEL, D_MODEL)
IN_DIM = sum(IN_WIDTHS)
IN_SPLITS = [int(s) for s in np.cumsum(IN_WIDTHS)[:-1]]

kernel_name = "hybrid_gmlp_ssd_macaron_block"


def rmsnorm(x, g):
    xf = x.astype(jnp.float32)
    y = xf * lax.rsqrt(jnp.mean(xf * xf, axis=-1, keepdims=True) + EPS)
    return (y * g.astype(jnp.float32)).astype(x.dtype)


def layernorm(x, g, b):
    xf = x.astype(jnp.float32)
    mu = jnp.mean(xf, axis=-1, keepdims=True)
    var = jnp.mean(jnp.square(xf - mu), axis=-1, keepdims=True)
    y = (xf - mu) * lax.rsqrt(var + EPS)
    return (y * g.astype(jnp.float32) + b.astype(jnp.float32)).astype(x.dtype)


def swiglu(x, w_in, w_out):
    gate, up = jnp.split(x @ w_in, 2, axis=-1)
    return (jax.nn.silu(gate) * up) @ w_out


def sgu_branch(u, v, ln_g, ln_b, w_s, b_s):
    bsz, L, _ = v.shape
    nblk = L // SGU_BLOCK
    v = layernorm(v, ln_g, ln_b).reshape(bsz, nblk, SGU_BLOCK, SGU_GROUPS, SGU_GDIM)
    cidx = np.arange(SGU_BLOCK) // CHUNK
    mask = (cidx[None, :] <= cidx[:, None])
    w = w_s * jnp.asarray(mask, w_s.dtype)[None]
    f = jnp.einsum("gts,bnsgc->bntgc", w, v) + b_s.T[None, None, :, :, None]
    return u * f.reshape(bsz, L, SGU_WIDTH)


def ssd_scan(xdt, adt, bm, cm):
    bsz, L = xdt.shape[:2]
    nc = L // SSM_CHUNK
    Q = SSM_CHUNK
    x = xdt.reshape(bsz, nc, Q, SSM_GROUPS, SSM_HPG, SSM_HEADDIM)
    a = adt.reshape(bsz, nc, Q, SSM_GROUPS, SSM_HPG)
    b = bm.reshape(bsz, nc, Q, SSM_GROUPS, SSM_STATE)
    c = cm.reshape(bsz, nc, Q, SSM_GROUPS, SSM_STATE)
    a_cs = jnp.cumsum(a, axis=2)
    tri = np.tril(np.ones((Q, Q), dtype=bool))[:, :, None, None]
    seg = a_cs[:, :, :, None] - a_cs[:, :, None, :]
    decay = jnp.exp(jnp.where(tri, seg, -jnp.inf))
    cb = jnp.einsum("bctgn,bcsgn->bctsg", c, b)
    y_diag = jnp.einsum("bctsg,bctsgk,bcsgkp->bctgkp", cb, decay, x)
    decay_to_end = jnp.exp(a_cs[:, :, -1:] - a_cs)
    states = jnp.einsum("bclgn,bclgk,bclgkp->bcgkpn", b, decay_to_end, x)
    chunk_decay = jnp.exp(a_cs[:, :, -1])

    def step(s, inp):
        st, dec = inp
        return s * dec[..., None, None] + st, s

    s0 = jnp.zeros((bsz, SSM_GROUPS, SSM_HPG, SSM_HEADDIM, SSM_STATE), x.dtype)
    _, prev = lax.scan(step, s0, (jnp.moveaxis(states, 1, 0), jnp.moveaxis(chunk_decay, 1, 0)))
    prev = jnp.moveaxis(prev, 0, 1)
    y_off = jnp.einsum("bctgn,bcgkpn,bctgk->bctgkp", c, prev, jnp.exp(a_cs))
    return (y_diag + y_off).reshape(bsz, L, SSM_HEADS, SSM_HEADDIM)


def mamba2_branch(z, xbc, dt_raw, conv_w, conv_b, dt_bias, a_log, d_skip, norm_g):
    bsz, L, _ = xbc.shape
    xbc = lax.conv_general_dilated(
        xbc, conv_w[:, None, :], window_strides=(1,), padding=[(SSM_CONV - 1, 0)],
        dimension_numbers=("NWC", "WIO", "NWC"), feature_group_count=SSM_CONV_DIM) + conv_b
    xbc = jax.nn.silu(xbc)
    xs, bm, cm = jnp.split(xbc, [SSM_INNER, SSM_INNER + SSM_GROUPS * SSM_STATE], axis=-1)
    xs = xs.reshape(bsz, L, SSM_HEADS, SSM_HEADDIM)
    bm = bm.reshape(bsz, L, SSM_GROUPS, SSM_STATE).astype(jnp.float32)
    cm = cm.reshape(bsz, L, SSM_GROUPS, SSM_STATE).astype(jnp.float32)
    dt = jax.nn.softplus(dt_raw.astype(jnp.float32) + dt_bias.astype(jnp.float32))
    a = -jnp.exp(a_log.astype(jnp.float32))
    y = ssd_scan(xs.astype(jnp.float32) * dt[..., None], dt * a, bm, cm)
    y = y + xs.astype(jnp.float32) * d_skip.astype(jnp.float32)[:, None]
    yg = (y.reshape(bsz, L, SSM_INNER) * jax.nn.silu(z.astype(jnp.float32)))
    yg = yg.reshape(bsz, L, SSM_GROUPS, SSM_INNER // SSM_GROUPS)
    yg = yg * lax.rsqrt(jnp.mean(yg * yg, axis=-1, keepdims=True) + EPS)
    return (yg.reshape(bsz, L, SSM_INNER) * norm_g.astype(jnp.float32)).astype(z.dtype)


def setup_inputs(seed: int = 0) -> dict:
    key = jax.random.key(seed)
    ks = jax.random.split(key, 24)
    f32 = jnp.float32
    nrm = lambda k, shape, s: jax.random.normal(k, shape, f32) * s
    gain = lambda k, shape: 1.0 + 0.02 * jax.random.normal(k, shape, f32)
    dt = jnp.exp(jax.random.uniform(ks[13], (DEPTH, SSM_HEADS), f32, np.log(1e-3), np.log(1e-1)))
    return {
        "x": jax.random.normal(ks[0], (BATCH, SEQ, D_MODEL), f32),
        "ffn1_norm": gain(ks[1], (DEPTH, D_MODEL)),
        "ffn1_w_in": nrm(ks[2], (DEPTH, D_MODEL, 2 * D_FF), D_MODEL ** -0.5),
        "ffn1_w_out": nrm(ks[3], (DEPTH, D_FF, D_MODEL), D_FF ** -0.5),
        "mix_norm": gain(ks[4], (DEPTH, D_MODEL)),
        "w_in": nrm(ks[5], (DEPTH, D_MODEL, IN_DIM), D_MODEL ** -0.5),
        "sgu_ln_g": gain(ks[6], (DEPTH, SGU_WIDTH)),
        "sgu_ln_b": nrm(ks[7], (DEPTH, SGU_WIDTH), 0.02),
        "sgu_w_s": nrm(ks[8], (DEPTH, SGU_GROUPS, SGU_BLOCK, SGU_BLOCK), SGU_BLOCK ** -0.5),
        "sgu_b_s": gain(ks[9], (DEPTH, SGU_GROUPS, SGU_BLOCK)),
        "conv_w": nrm(ks[10], (DEPTH, SSM_CONV, SSM_CONV_DIM), SSM_CONV ** -0.5),
        "conv_b": nrm(ks[11], (DEPTH, SSM_CONV_DIM), 0.02),
        "dt_bias": dt + jnp.log(-jnp.expm1(-dt)),
        "a_log": jnp.log(jax.random.uniform(ks[12], (DEPTH, SSM_HEADS), f32, 1.0, 16.0)),
        "d_skip": gain(ks[14], (DEPTH, SSM_HEADS)),
        "ssm_norm": gain(ks[15], (DEPTH, SSM_INNER)),
        "w_a": nrm(ks[16], (DEPTH, SGU_WIDTH, D_MODEL), SGU_WIDTH ** -0.5),
        "w_b": nrm(ks[17], (DEPTH, SSM_INNER, D_MODEL), SSM_INNER ** -0.5),
        "w_o": nrm(ks[18], (DEPTH, D_MODEL, D_MODEL), D_MODEL ** -0.5),
        "ffn2_norm": gain(ks[19], (DEPTH, D_MODEL)),
        "ffn2_w_in": nrm(ks[20], (DEPTH, D_MODEL, 2 * D_FF), D_MODEL ** -0.5),
        "ffn2_w_out": nrm(ks[21], (DEPTH, D_FF, D_MODEL), D_FF ** -0.5),
        "final_norm": gain(ks[22], (D_MODEL,)),
    }


def reference(x, ffn1_norm, ffn1_w_in, ffn1_w_out, mix_norm, w_in, sgu_ln_g, sgu_ln_b,
              sgu_w_s, sgu_b_s, conv_w, conv_b, dt_bias, a_log, d_skip, ssm_norm,
              w_a, w_b, w_o, ffn2_norm, ffn2_w_in, ffn2_w_out, final_norm):
    h = x
    for l in range(DEPTH):
        h = h + 0.5 * swiglu(rmsnorm(h, ffn1_norm[l]), ffn1_w_in[l], ffn1_w_out[l])
        n = rmsnorm(h, mix_norm[l])
        u, v, z, xbc, dt_raw, g_a, g_b = jnp.split(n @ w_in[l], IN_SPLITS, axis=-1)
        y_a = sgu_branch(jax.nn.gelu(u, approximate=False), jax.nn.gelu(v, approximate=False),
                         sgu_ln_g[l], sgu_ln_b[l], sgu_w_s[l], sgu_b_s[l]) @ w_a[l]
        y_b = mamba2_branch(z, xbc, dt_raw, conv_w[l], conv_b[l], dt_bias[l], a_log[l],
                            d_skip[l], ssm_norm[l]) @ w_b[l]
        merged = jax.nn.sigmoid(g_a) * y_a + jax.nn.sigmoid(g_b) * y_b
        h = h + merged @ w_o[l]
        h = h + 0.5 * swiglu(rmsnorm(h, ffn2_norm[l]), ffn2_w_in[l], ffn2_w_out[l])
    return rmsnorm(h, final_norm)
```

```python
import functools

import jax
import jax.numpy as jnp
import numpy as np
from jax import lax
from jax.experimental import pallas as pl
from jax.experimental.pallas import tpu as pltpu

F32 = jnp.float32
BF16 = jnp.bfloat16

LANES = 128
VMEM_LIMIT_BYTES = 56 * 1024 * 1024

D_MODEL = 1024
D_FF = 4 * D_MODEL
EPS = 1e-6
SGU_BLOCK = 128
SGU_GROUPS = 8
SGU_CAUSAL = 64
SSM_INNER = 2 * D_MODEL
SSM_HEADDIM = 64
SSM_HEADS = SSM_INNER // SSM_HEADDIM
SSM_GROUPS = 8
SSM_HPG = SSM_HEADS // SSM_GROUPS
SSM_STATE = 128
SSM_CONV = 4
SSM_CONV_DIM = SSM_INNER + 2 * SSM_GROUPS * SSM_STATE
GROUP_W = SSM_HPG * SSM_HEADDIM

SSD_Q = 128

U0 = 0
V0 = U0 + D_MODEL
Z0 = V0 + D_MODEL
XBC0 = Z0 + SSM_INNER
GA0 = XBC0 + SSM_CONV_DIM
GB0 = GA0 + D_MODEL
DT0 = GB0 + D_MODEL
P_DIM = DT0 + LANES

CONV_TAIL = 8

FFN_TM = 512
FFN_TF = 512
PROJ_TM = 1024
PROJ_TN = 1152
MIX_TL = 256


def _rms(x, g):
    ms = jnp.mean(x * x, axis=-1, keepdims=True)
    return x * lax.rsqrt(ms + EPS) * g


def _silu(x):
    return x * jax.nn.sigmoid(x)


def _gelu_erf(x):
    return 0.5 * x * (1.0 + lax.erf(x * np.float32(1.0 / np.sqrt(2.0))))


def _dot(a, b):
    return jnp.dot(a, b, preferred_element_type=F32)


def _split3(x):
    hi = x.astype(BF16)
    r1 = x - hi.astype(F32)
    mid = r1.astype(BF16)
    lo = (r1 - mid.astype(F32)).astype(BF16)
    return hi, mid, lo


def _dot_exact_rhs(a_bf16_exact, x):
    hi, mid, lo = _split3(x)
    return _dot(a_bf16_exact, hi) + _dot(a_bf16_exact, mid) + _dot(a_bf16_exact, lo)


def _dot_exact_lhs(x, b_bf16_exact):
    hi, mid, lo = _split3(x)
    return _dot(hi, b_bf16_exact) + _dot(mid, b_bf16_exact) + _dot(lo, b_bf16_exact)


def _ffn_kernel(x_ref, g_ref, win_ref, wout_ref, g2_ref, *refs, emit_h):
    if emit_h:
        h_ref, n_ref, acc_ref = refs
    else:
        n_ref, acc_ref = refs
    x = x_ref[...]
    n = _rms(x, g_ref[...]).astype(BF16)
    for c in range(D_FF // FFN_TF):
        lo, hi = c * FFN_TF, (c + 1) * FFN_TF
        gate = _dot(n, win_ref[:, lo:hi])
        up = _dot(n, win_ref[:, D_FF + lo:D_FF + hi])
        act = (_silu(gate) * up).astype(BF16)
        part = _dot(act, wout_ref[lo:hi, :])
        if c == 0:
            acc_ref[...] = part
        else:
            acc_ref[...] += part
    h = x + 0.5 * acc_ref[...]
    if emit_h:
        h_ref[...] = h
    n_ref[...] = _rms(h, g2_ref[...]).astype(n_ref.dtype)


def _ffn_call(x2, g, w_in, w_out, g2, *, emit_h, norm_dtype):
    t = x2.shape[0]
    const = lambda i: (0, 0)
    row = lambda i: (i, 0)
    single = pl.Buffered(1)
    out_shape = [jax.ShapeDtypeStruct((t, D_MODEL), norm_dtype)]
    out_specs = [pl.BlockSpec((FFN_TM, D_MODEL), row)]
    if emit_h:
        out_shape.insert(0, jax.ShapeDtypeStruct((t, D_MODEL), F32))
        out_specs.insert(0, pl.BlockSpec((FFN_TM, D_MODEL), row))
    return pl.pallas_call(
        functools.partial(_ffn_kernel, emit_h=emit_h),
        grid=(t // FFN_TM,),
        in_specs=[
            pl.BlockSpec((FFN_TM, D_MODEL), row),
            pl.BlockSpec((1, D_MODEL), const),
            pl.BlockSpec((D_MODEL, 2 * D_FF), const, pipeline_mode=single),
            pl.BlockSpec((D_FF, D_MODEL), const, pipeline_mode=single),
            pl.BlockSpec((1, D_MODEL), const),
        ],
        out_specs=out_specs,
        out_shape=out_shape,
        scratch_shapes=[pltpu.VMEM((FFN_TM, D_MODEL), F32)],
        compiler_params=pltpu.CompilerParams(
            dimension_semantics=("parallel",), vmem_limit_bytes=VMEM_LIMIT_BYTES),
        name="ffn_h" if emit_h else "ffn_final",
    )(x2, g, w_in, w_out, g2)


def _proj_kernel(n_ref, w_ref, o_ref):
    o_ref[...] = _dot(n_ref[...], w_ref[...])


def _proj_call(n2, w_cat):
    t = n2.shape[0]
    return pl.pallas_call(
        _proj_kernel,
        grid=(t // PROJ_TM, P_DIM // PROJ_TN),
        in_specs=[
            pl.BlockSpec((PROJ_TM, D_MODEL), lambda i, j: (i, 0)),
            pl.BlockSpec((D_MODEL, PROJ_TN), lambda i, j: (0, j)),
        ],
        out_specs=pl.BlockSpec((PROJ_TM, PROJ_TN), lambda i, j: (i, j)),
        out_shape=jax.ShapeDtypeStruct((t, P_DIM), F32),
        compiler_params=pltpu.CompilerParams(
            dimension_semantics=("parallel", "arbitrary"), vmem_limit_bytes=VMEM_LIMIT_BYTES),
        name="inproj",
    )(n2, w_cat)


def _mixer_kernel(p_ref, h1_ref, lng_ref, lnb_ref, ws_ref, bst_ref, cw_ref, cb_ref,
                  dtb_ref, alog_ref, dskip_ref, nrm_ref, wa_ref, wb_ref, wo_ref,
                  out_ref, state_ref, ext_ref, ya_ref, yb_ref):
    tl = p_ref.shape[0]

    @pl.when(pl.program_id(1) == 0)
    def _():
        state_ref[...] = jnp.zeros_like(state_ref)
        ext_ref[0:CONV_TAIL, :] = jnp.zeros((CONV_TAIL, SSM_CONV_DIM), F32)

    ext_ref[CONV_TAIL:CONV_TAIL + tl, :] = p_ref[:, XBC0:XBC0 + SSM_CONV_DIM]

    row = lax.broadcasted_iota(jnp.int32, (SSD_Q, SSD_Q), 0)
    col = lax.broadcasted_iota(jnp.int32, (SSD_Q, SSD_Q), 1)
    causal = row >= col
    tri_b = causal.astype(F32).astype(BF16)
    sgu_mask = ((col // SGU_CAUSAL) <= (row // SGU_CAUSAL)).astype(F32)
    e_row = lax.broadcasted_iota(jnp.int32, (LANES, SSM_INNER), 0)
    e_col = lax.broadcasted_iota(jnp.int32, (LANES, SSM_INNER), 1)
    expand_b = (e_col // SSM_HEADDIM == e_row).astype(F32).astype(BF16)
    lane = lax.broadcasted_iota(jnp.int32, (2 * SSD_Q, LANES), 1)
    even_half = lane < SSM_HEADDIM
    a_neg = -jnp.exp(alog_ref[...])

    for r in range(tl // SSD_Q):
        r0 = r * SSD_Q
        rows = pl.ds(r0, SSD_Q)

        u = _gelu_erf(p_ref[rows, U0:U0 + D_MODEL])
        v = _gelu_erf(p_ref[rows, V0:V0 + D_MODEL])
        mu = jnp.mean(v, axis=-1, keepdims=True)
        vc = v - mu
        var = jnp.mean(vc * vc, axis=-1, keepdims=True)
        vln = (vc * lax.rsqrt(var + EPS) * lng_ref[...] + lnb_ref[...]).astype(BF16)
        f_parts = []
        for g in range(SGU_GROUPS):
            wm = (ws_ref[g] * sgu_mask).astype(BF16)
            fg = _dot(wm, vln[:, g * LANES:(g + 1) * LANES]) + bst_ref[:, g:g + 1]
            f_parts.append(fg)
        f = jnp.concatenate(f_parts, axis=1)
        ya_ref[rows, :] = (u * f).astype(BF16)

        conv = cb_ref[...] + cw_ref[0:1, :] * ext_ref[pl.ds(r0 + CONV_TAIL - 3, SSD_Q), :]
        for k in range(1, SSM_CONV):
            conv = conv + cw_ref[k:k + 1, :] * ext_ref[pl.ds(r0 + CONV_TAIL - 3 + k, SSD_Q), :]
        xbc = _silu(conv)
        xs = xbc[:, 0:SSM_INNER]
        bm = xbc[:, SSM_INNER:SSM_INNER + SSM_GROUPS * SSM_STATE].astype(BF16)
        cm = xbc[:, SSM_INNER + SSM_GROUPS * SSM_STATE:]
        cm_b = cm.astype(BF16)
        xs_b = xs.astype(BF16)

        dt = jax.nn.softplus(p_ref[rows, DT0:DT0 + LANES] + dtb_ref[...])
        acs = _dot_exact_rhs(tri_b, dt * a_neg)
        acs_t = acs.T
        dt_t = dt.T
        ea = jnp.exp(acs)
        last = acs[SSD_Q - 1:SSD_Q, :]
        wgt = dt * jnp.exp(last - acs)
        cdec = jnp.broadcast_to(jnp.exp(last), (8, LANES))
        expd = _dot_exact_lhs(jnp.concatenate([wgt, cdec], axis=0), expand_b)
        xw = (xs * expd[0:SSD_Q, :]).astype(BF16)
        cdec_x = expd[SSD_Q:SSD_Q + 1, :]

        y_parts = []
        for g in range(SSM_GROUPS):
            gs = slice(g * SSM_STATE, (g + 1) * SSM_STATE)
            cb = lax.dot_general(cm_b[:, gs], bm[:, gs], (((1,), (1,)), ((), ())),
                                 preferred_element_type=F32)
            st = state_ref[g]
            st_b = st.astype(BF16)
            cm_g = cm[:, gs]
            for j in range(SSM_HPG // 2):
                lhs_parts = []
                for k in (2 * j, 2 * j + 1):
                    h = g * SSM_HPG + k
                    seg = acs[:, h:h + 1] - acs_t[h:h + 1, :]
                    decay = jnp.exp(jnp.where(causal, seg, -jnp.inf))
                    lhs_parts.append((cb * decay * dt_t[h:h + 1, :]).astype(BF16))
                    lhs_parts.append((cm_g * ea[:, h:h + 1]).astype(BF16))
                c0 = g * GROUP_W + j * LANES
                pair = jnp.concatenate(
                    [xs_b[:, c0:c0 + LANES], st_b[:, j * LANES:(j + 1) * LANES]], axis=0)
                zero = jnp.zeros_like(pair)
                rhs = jnp.concatenate(
                    [jnp.where(even_half, pair, zero), jnp.where(even_half, zero, pair)], axis=0)
                y_parts.append(_dot(jnp.concatenate(lhs_parts, axis=1), rhs))
            new = lax.dot_general(bm[:, gs], xw[:, g * GROUP_W:(g + 1) * GROUP_W],
                                  (((0,), (0,)), ((), ())), preferred_element_type=F32)
            state_ref[g] = st * cdec_x[:, g * GROUP_W:(g + 1) * GROUP_W] + new
        y = jnp.concatenate(y_parts, axis=1) + xs * dskip_ref[...]

        yg = y * _silu(p_ref[rows, Z0:Z0 + SSM_INNER])
        n_parts = []
        for g in range(SSM_GROUPS):
            blk = yg[:, g * GROUP_W:(g + 1) * GROUP_W]
            ms = jnp.mean(blk * blk, axis=-1, keepdims=True)
            n_parts.append(blk * lax.rsqrt(ms + EPS))
        yb_ref[rows, :] = (jnp.concatenate(n_parts, axis=1) * nrm_ref[...]).astype(BF16)

    ext_ref[0:CONV_TAIL, :] = ext_ref[tl:tl + CONV_TAIL, :]

    y_a = _dot(ya_ref[...], wa_ref[...])
    y_b = _dot(yb_ref[...], wb_ref[...])
    merged = (jax.nn.sigmoid(p_ref[:, GA0:GA0 + D_MODEL]) * y_a
              + jax.nn.sigmoid(p_ref[:, GB0:GB0 + D_MODEL]) * y_b)
    out_ref[...] = h1_ref[...] + _dot(merged.astype(BF16), wo_ref[...])


def _mixer_call(p3, h1_3, lng, lnb, w_s, b_st, conv_w, conv_b, dtb, alog, dskip, nrm, w_a, w_b, w_o):
    b, l, _ = p3.shape
    tl = min(MIX_TL, l)
    const2 = lambda bi, i: (0, 0)
    const3 = lambda bi, i: (0, 0, 0)
    tile = lambda bi, i: (bi, i, 0)
    single = pl.Buffered(1)
    return pl.pallas_call(
        _mixer_kernel,
        grid=(b, l // tl),
        in_specs=[
            pl.BlockSpec((None, tl, P_DIM), tile),
            pl.BlockSpec((None, tl, D_MODEL), tile),
            pl.BlockSpec((1, D_MODEL), const2),
            pl.BlockSpec((1, D_MODEL), const2),
            pl.BlockSpec((SGU_GROUPS, SGU_BLOCK, SGU_BLOCK), const3),
            pl.BlockSpec((SGU_BLOCK, SGU_GROUPS), const2),
            pl.BlockSpec((SSM_CONV, SSM_CONV_DIM), const2),
            pl.BlockSpec((1, SSM_CONV_DIM), const2),
            pl.BlockSpec((1, LANES), const2),
            pl.BlockSpec((1, LANES), const2),
            pl.BlockSpec((1, SSM_INNER), const2),
            pl.BlockSpec((1, SSM_INNER), const2),
            pl.BlockSpec((D_MODEL, D_MODEL), const2, pipeline_mode=single),
            pl.BlockSpec((SSM_INNER, D_MODEL), const2, pipeline_mode=single),
            pl.BlockSpec((D_MODEL, D_MODEL), const2, pipeline_mode=single),
        ],
        out_specs=pl.BlockSpec((None, tl, D_MODEL), tile),
        out_shape=jax.ShapeDtypeStruct((b, l, D_MODEL), F32),
        scratch_shapes=[
            pltpu.VMEM((SSM_GROUPS, SSM_STATE, GROUP_W), F32),
            pltpu.VMEM((tl + CONV_TAIL, SSM_CONV_DIM), F32),
            pltpu.VMEM((tl, D_MODEL), BF16),
            pltpu.VMEM((tl, SSM_INNER), BF16),
        ],
        compiler_params=pltpu.CompilerParams(
            dimension_semantics=("parallel", "arbitrary"), vmem_limit_bytes=VMEM_LIMIT_BYTES),
        name="mixer",
    )(p3, h1_3, lng, lnb, w_s, b_st, conv_w, conv_b, dtb, alog, dskip, nrm, w_a, w_b, w_o)


def _regroup_w_in(w):
    dt0 = 2 * D_MODEL + SSM_INNER + SSM_CONV_DIM
    head, dt, gates = w[:, :dt0], w[:, dt0:dt0 + SSM_HEADS], w[:, dt0 + SSM_HEADS:]
    pad = jnp.zeros((w.shape[0], LANES - SSM_HEADS), w.dtype)
    return jnp.concatenate([head, gates, dt, pad], axis=1)


def _pad_lanes(v):
    return jnp.concatenate([v, jnp.zeros((LANES - v.shape[0],), v.dtype)]).reshape(1, LANES)


def kernel(x, ffn1_norm, ffn1_w_in, ffn1_w_out, mix_norm, w_in, sgu_ln_g, sgu_ln_b, sgu_w_s, sgu_b_s,
           conv_w, conv_b, dt_bias, a_log, d_skip, ssm_norm, w_a, w_b, w_o, ffn2_norm, ffn2_w_in,
           ffn2_w_out, final_norm):
    b, l, d = x.shape
    assert ffn1_norm.shape[0] == 1, "single-layer block"
    assert d == D_MODEL and l % SSD_Q == 0 and (b * l) % max(FFN_TM, PROJ_TM) == 0
    row = lambda v: v.reshape(1, -1).astype(F32)
    h1, n2 = _ffn_call(x.reshape(b * l, d), row(ffn1_norm[0]), ffn1_w_in[0].astype(BF16),
                       ffn1_w_out[0].astype(BF16), row(mix_norm[0]), emit_h=True, norm_dtype=BF16)
    p = _proj_call(n2, _regroup_w_in(w_in[0]).astype(BF16))
    h2 = _mixer_call(
        p.reshape(b, l, P_DIM), h1.reshape(b, l, d),
        row(sgu_ln_g[0]), row(sgu_ln_b[0]), sgu_w_s[0], sgu_b_s[0].T,
        conv_w[0], row(conv_b[0]), _pad_lanes(dt_bias[0]), _pad_lanes(a_log[0]),
        row(jnp.repeat(d_skip[0], SSM_HEADDIM)), row(ssm_norm[0]),
        w_a[0].astype(BF16), w_b[0].astype(BF16), w_o[0].astype(BF16))
    (out,) = _ffn_call(h2.reshape(b * l, d), row(ffn2_norm[0]), ffn2_w_in[0].astype(BF16),
                       ffn2_w_out[0].astype(BF16), row(final_norm), emit_h=False, norm_dtype=F32)
    return out.reshape(b, l, d)
```

```python
import functools

import jax
import jax.numpy as jnp
import numpy as np
from jax import lax
from jax.experimental import pallas as pl
from jax.experimental.pallas import tpu as pltpu

F32 = jnp.float32
BF16 = jnp.bfloat16

LANES = 128
VMEM_LIMIT_BYTES = 56 * 1024 * 1024

D_MODEL = 1024
D_FF = 4 * D_MODEL
EPS = 1e-6
SGU_BLOCK = 128
SGU_GROUPS = 8
SGU_CAUSAL = 64
SSM_INNER = 2 * D_MODEL
SSM_HEADDIM = 64
SSM_HEADS = SSM_INNER // SSM_HEADDIM
SSM_GROUPS = 8
SSM_HPG = SSM_HEADS // SSM_GROUPS
SSM_STATE = 128
SSM_CONV = 4
SSM_CONV_DIM = SSM_INNER + 2 * SSM_GROUPS * SSM_STATE
GROUP_W = SSM_HPG * SSM_HEADDIM

SSD_Q = 128

U0 = 0
V0 = U0 + D_MODEL
Z0 = V0 + D_MODEL
GA0 = Z0 + SSM_INNER
GB0 = GA0 + D_MODEL
DT0 = GB0 + D_MODEL
XBC0 = DT0 + LANES
P_DIM = XBC0 + SSM_CONV_DIM

CONV_TAIL = 8

FFN_TM = 512
FFN_TF = 512
MIX_TL = 256


def _rms(x, g):
    ms = jnp.mean(x * x, axis=-1, keepdims=True)
    return x * lax.rsqrt(ms + EPS) * g


def _silu(x):
    return x * jax.nn.sigmoid(x)


def _gelu_erf(x):
    return 0.5 * x * (1.0 + lax.erf(x * np.float32(1.0 / np.sqrt(2.0))))


def _dot(a, b):
    return jnp.dot(a, b, preferred_element_type=F32)


def _split3(x):
    hi = x.astype(BF16)
    r1 = x - hi.astype(F32)
    mid = r1.astype(BF16)
    lo = (r1 - mid.astype(F32)).astype(BF16)
    return hi, mid, lo


def _dot_exact_rhs(a_bf16_exact, x):
    hi, mid, lo = _split3(x)
    return _dot(a_bf16_exact, hi) + _dot(a_bf16_exact, mid) + _dot(a_bf16_exact, lo)


def _dot_exact_lhs(x, b_bf16_exact):
    hi, mid, lo = _split3(x)
    return _dot(hi, b_bf16_exact) + _dot(mid, b_bf16_exact) + _dot(lo, b_bf16_exact)


def _ffn_kernel(x_ref, g_ref, win_ref, wout_ref, g2_ref, *refs, emit_h):
    if emit_h:
        h_ref, n_ref, acc_ref = refs
    else:
        n_ref, acc_ref = refs
    x = x_ref[...]
    n = _rms(x, g_ref[...]).astype(BF16)
    for c in range(D_FF // FFN_TF):
        lo, hi = c * FFN_TF, (c + 1) * FFN_TF
        gate = _dot(n, win_ref[:, lo:hi])
        up = _dot(n, win_ref[:, D_FF + lo:D_FF + hi])
        act = (_silu(gate) * up).astype(BF16)
        part = _dot(act, wout_ref[lo:hi, :])
        if c == 0:
            acc_ref[...] = part
        else:
            acc_ref[...] += part
    h = x + 0.5 * acc_ref[...]
    if emit_h:
        h_ref[...] = h
    n_ref[...] = _rms(h, g2_ref[...]).astype(n_ref.dtype)


def _ffn_call(x2, g, w_in, w_out, g2, *, emit_h, norm_dtype):
    t = x2.shape[0]
    const = lambda i: (0, 0)
    row = lambda i: (i, 0)
    single = pl.Buffered(1)
    out_shape = [jax.ShapeDtypeStruct((t, D_MODEL), norm_dtype)]
    out_specs = [pl.BlockSpec((FFN_TM, D_MODEL), row)]
    if emit_h:
        out_shape.insert(0, jax.ShapeDtypeStruct((t, D_MODEL), F32))
        out_specs.insert(0, pl.BlockSpec((FFN_TM, D_MODEL), row))
    return pl.pallas_call(
        functools.partial(_ffn_kernel, emit_h=emit_h),
        grid=(t // FFN_TM,),
        in_specs=[
            pl.BlockSpec((FFN_TM, D_MODEL), row),
            pl.BlockSpec((1, D_MODEL), const),
            pl.BlockSpec((D_MODEL, 2 * D_FF), const, pipeline_mode=single),
            pl.BlockSpec((D_FF, D_MODEL), const, pipeline_mode=single),
            pl.BlockSpec((1, D_MODEL), const),
        ],
        out_specs=out_specs,
        out_shape=out_shape,
        scratch_shapes=[pltpu.VMEM((FFN_TM, D_MODEL), F32)],
        compiler_params=pltpu.CompilerParams(
            dimension_semantics=("parallel",), vmem_limit_bytes=VMEM_LIMIT_BYTES),
        name="ffn_h" if emit_h else "ffn_final",
    )(x2, g, w_in, w_out, g2)


def _mixer_kernel(n_ref, win_ref, h1_ref, lng_ref, lnb_ref, ws_ref, bst_ref, cw_ref, cb_ref,
                  dtb_ref, alog_ref, dskip_ref, nrm_ref, wa_ref, wb_ref, wo_ref,
                  out_ref, state_ref, ext_ref, p_ref, ya_ref, yb_ref):
    tl = n_ref.shape[0]

    @pl.when(pl.program_id(1) == 0)
    def _():
        state_ref[...] = jnp.zeros_like(state_ref)
        ext_ref[0:CONV_TAIL, :] = jnp.zeros((CONV_TAIL, SSM_CONV_DIM), F32)

    n2 = n_ref[...]
    p_ref[:, U0:Z0] = _dot(n2, win_ref[:, U0:Z0])
    ext_ref[CONV_TAIL:CONV_TAIL + tl, :] = _dot(n2, win_ref[:, XBC0:P_DIM])
    p_ref[:, Z0:GA0] = _dot(n2, win_ref[:, Z0:GA0])
    p_ref[:, GA0:XBC0] = _dot(n2, win_ref[:, GA0:XBC0])

    row = lax.broadcasted_iota(jnp.int32, (SSD_Q, SSD_Q), 0)
    col = lax.broadcasted_iota(jnp.int32, (SSD_Q, SSD_Q), 1)
    causal = row >= col
    tri_b = causal.astype(F32).astype(BF16)
    sgu_mask = ((col // SGU_CAUSAL) <= (row // SGU_CAUSAL)).astype(F32)
    e_row = lax.broadcasted_iota(jnp.int32, (LANES, SSM_INNER), 0)
    e_col = lax.broadcasted_iota(jnp.int32, (LANES, SSM_INNER), 1)
    expand_b = (e_col // SSM_HEADDIM == e_row).astype(F32).astype(BF16)
    lane = lax.broadcasted_iota(jnp.int32, (2 * SSD_Q, LANES), 1)
    even_half = lane < SSM_HEADDIM
    a_neg = -jnp.exp(alog_ref[...])

    for r in range(tl // SSD_Q):
        r0 = r * SSD_Q
        rows = pl.ds(r0, SSD_Q)

        u = _gelu_erf(p_ref[rows, U0:U0 + D_MODEL])
        v = _gelu_erf(p_ref[rows, V0:V0 + D_MODEL])
        mu = jnp.mean(v, axis=-1, keepdims=True)
        vc = v - mu
        var = jnp.mean(vc * vc, axis=-1, keepdims=True)
        vln = (vc * lax.rsqrt(var + EPS) * lng_ref[...] + lnb_ref[...]).astype(BF16)
        f_parts = []
        for g in range(SGU_GROUPS):
            wm = (ws_ref[g] * sgu_mask).astype(BF16)
            fg = _dot(wm, vln[:, g * LANES:(g + 1) * LANES]) + bst_ref[:, g:g + 1]
            f_parts.append(fg)
        f = jnp.concatenate(f_parts, axis=1)
        ya_ref[rows, :] = (u * f).astype(BF16)

        conv = cb_ref[...] + cw_ref[0:1, :] * ext_ref[pl.ds(r0 + CONV_TAIL - 3, SSD_Q), :]
        for k in range(1, SSM_CONV):
            conv = conv + cw_ref[k:k + 1, :] * ext_ref[pl.ds(r0 + CONV_TAIL - 3 + k, SSD_Q), :]
        xbc = _silu(conv)
        xs = xbc[:, 0:SSM_INNER]
        bm = xbc[:, SSM_INNER:SSM_INNER + SSM_GROUPS * SSM_STATE].astype(BF16)
        cm = xbc[:, SSM_INNER + SSM_GROUPS * SSM_STATE:]
        cm_b = cm.astype(BF16)
        xs_b = xs.astype(BF16)

        dt = jax.nn.softplus(p_ref[rows, DT0:DT0 + LANES] + dtb_ref[...])
        acs = _dot_exact_rhs(tri_b, dt * a_neg)
        acs_t = acs.T
        dt_t = dt.T
        ea = jnp.exp(acs)
        last = acs[SSD_Q - 1:SSD_Q, :]
        wgt = dt * jnp.exp(last - acs)
        cdec = jnp.broadcast_to(jnp.exp(last), (16, LANES))
        expd = _dot_exact_lhs(jnp.concatenate([wgt, cdec], axis=0), expand_b)
        xw = (xs * expd[0:SSD_Q, :]).astype(BF16)
        cdec_x = expd[SSD_Q:SSD_Q + 1, :]

        y_parts = []
        for g in range(SSM_GROUPS):
            gs = slice(g * SSM_STATE, (g + 1) * SSM_STATE)
            cb = lax.dot_general(cm_b[:, gs], bm[:, gs], (((1,), (1,)), ((), ())),
                                 preferred_element_type=F32)
            st = state_ref[g]
            st_b = st.astype(BF16)
            cm_g = cm[:, gs]
            for j in range(SSM_HPG // 2):
                lhs_parts = []
                for k in (2 * j, 2 * j + 1):
                    h = g * SSM_HPG + k
                    seg = acs[:, h:h + 1] - acs_t[h:h + 1, :]
                    decay = jnp.exp(jnp.where(causal, seg, -jnp.inf))
                    lhs_parts.append((cb * decay * dt_t[h:h + 1, :]).astype(BF16))
                    lhs_parts.append((cm_g * ea[:, h:h + 1]).astype(BF16))
                c0 = g * GROUP_W + j * LANES
                pair = jnp.concatenate(
                    [xs_b[:, c0:c0 + LANES], st_b[:, j * LANES:(j + 1) * LANES]], axis=0)
                zero = jnp.zeros_like(pair)
                rhs = jnp.concatenate(
                    [jnp.where(even_half, pair, zero), jnp.where(even_half, zero, pair)], axis=0)
                y_parts.append(_dot(jnp.concatenate(lhs_parts, axis=1), rhs))
            new = lax.dot_general(bm[:, gs], xw[:, g * GROUP_W:(g + 1) * GROUP_W],
                                  (((0,), (0,)), ((), ())), preferred_element_type=F32)
            state_ref[g] = st * cdec_x[:, g * GROUP_W:(g + 1) * GROUP_W] + new
        y = jnp.concatenate(y_parts, axis=1) + xs * dskip_ref[...]

        yg = y * _silu(p_ref[rows, Z0:Z0 + SSM_INNER])
        n_parts = []
        for g in range(SSM_GROUPS):
            blk = yg[:, g * GROUP_W:(g + 1) * GROUP_W]
            ms = jnp.mean(blk * blk, axis=-1, keepdims=True)
            n_parts.append(blk * lax.rsqrt(ms + EPS))
        yb_ref[rows, :] = (jnp.concatenate(n_parts, axis=1) * nrm_ref[...]).astype(BF16)

    ext_ref[0:CONV_TAIL, :] = ext_ref[tl:tl + CONV_TAIL, :]

    y_a = _dot(ya_ref[...], wa_ref[...])
    y_b = _dot(yb_ref[...], wb_ref[...])
    merged = (jax.nn.sigmoid(p_ref[:, GA0:GA0 + D_MODEL]) * y_a
              + jax.nn.sigmoid(p_ref[:, GB0:GB0 + D_MODEL]) * y_b)
    out_ref[...] = h1_ref[...] + _dot(merged.astype(BF16), wo_ref[...])


def _mixer_call(n3, w_cat, h1_3, lng, lnb, w_s, b_st, conv_w, conv_b, dtb, alog, dskip, nrm, w_a, w_b, w_o):
    b, l, _ = n3.shape
    tl = min(MIX_TL, l)
    const2 = lambda bi, i: (0, 0)
    const3 = lambda bi, i: (0, 0, 0)
    tile = lambda bi, i: (bi, i, 0)
    single = pl.Buffered(1)
    return pl.pallas_call(
        _mixer_kernel,
        grid=(b, l // tl),
        in_specs=[
            pl.BlockSpec((None, tl, D_MODEL), tile),
            pl.BlockSpec((D_MODEL, P_DIM), const2, pipeline_mode=single),
            pl.BlockSpec((None, tl, D_MODEL), tile),
            pl.BlockSpec((1, D_MODEL), const2),
            pl.BlockSpec((1, D_MODEL), const2),
            pl.BlockSpec((SGU_GROUPS, SGU_BLOCK, SGU_BLOCK), const3),
            pl.BlockSpec((SGU_BLOCK, SGU_GROUPS), const2),
            pl.BlockSpec((SSM_CONV, SSM_CONV_DIM), const2),
            pl.BlockSpec((1, SSM_CONV_DIM), const2),
            pl.BlockSpec((1, LANES), const2),
            pl.BlockSpec((1, LANES), const2),
            pl.BlockSpec((1, SSM_INNER), const2),
            pl.BlockSpec((1, SSM_INNER), const2),
            pl.BlockSpec((D_MODEL, D_MODEL), const2, pipeline_mode=single),
            pl.BlockSpec((SSM_INNER, D_MODEL), const2, pipeline_mode=single),
            pl.BlockSpec((D_MODEL, D_MODEL), const2, pipeline_mode=single),
        ],
        out_specs=pl.BlockSpec((None, tl, D_MODEL), tile),
        out_shape=jax.ShapeDtypeStruct((b, l, D_MODEL), F32),
        scratch_shapes=[
            pltpu.VMEM((SSM_GROUPS, SSM_STATE, GROUP_W), F32),
            pltpu.VMEM((tl + CONV_TAIL, SSM_CONV_DIM), F32),
            pltpu.VMEM((tl, XBC0), F32),
            pltpu.VMEM((tl, D_MODEL), BF16),
            pltpu.VMEM((tl, SSM_INNER), BF16),
        ],
        compiler_params=pltpu.CompilerParams(
            dimension_semantics=("parallel", "arbitrary"), vmem_limit_bytes=VMEM_LIMIT_BYTES),
        name="mixer",
    )(n3, w_cat, h1_3, lng, lnb, w_s, b_st, conv_w, conv_b, dtb, alog, dskip, nrm, w_a, w_b, w_o)


def _regroup_w_in(w):
    xbc0 = 2 * D_MODEL + SSM_INNER
    dt0 = xbc0 + SSM_CONV_DIM
    uvz, xbc, dt, gates = w[:, :xbc0], w[:, xbc0:dt0], w[:, dt0:dt0 + SSM_HEADS], w[:, dt0 + SSM_HEADS:]
    pad = jnp.zeros((w.shape[0], LANES - SSM_HEADS), w.dtype)
    return jnp.concatenate([uvz, gates, dt, pad, xbc], axis=1)


def _pad_lanes(v):
    return jnp.concatenate([v, jnp.zeros((LANES - v.shape[0],), v.dtype)]).reshape(1, LANES)


def kernel(x, ffn1_norm, ffn1_w_in, ffn1_w_out, mix_norm, w_in, sgu_ln_g, sgu_ln_b, sgu_w_s, sgu_b_s,
           conv_w, conv_b, dt_bias, a_log, d_skip, ssm_norm, w_a, w_b, w_o, ffn2_norm, ffn2_w_in,
           ffn2_w_out, final_norm):
    b, l, d = x.shape
    assert ffn1_norm.shape[0] == 1, "single-layer block"
    assert d == D_MODEL and l % SSD_Q == 0 and (b * l) % FFN_TM == 0
    row = lambda v: v.reshape(1, -1).astype(F32)
    h1, n2 = _ffn_call(x.reshape(b * l, d), row(ffn1_norm[0]), ffn1_w_in[0].astype(BF16),
                       ffn1_w_out[0].astype(BF16), row(mix_norm[0]), emit_h=True, norm_dtype=BF16)
    h2 = _mixer_call(
        n2.reshape(b, l, d), _regroup_w_in(w_in[0]).astype(BF16), h1.reshape(b, l, d),
        row(sgu_ln_g[0]), row(sgu_ln_b[0]), sgu_w_s[0], sgu_b_s[0].T,
        conv_w[0], row(conv_b[0]), _pad_lanes(dt_bias[0]), _pad_lanes(a_log[0]),
        row(jnp.repeat(d_skip[0], SSM_HEADDIM)), row(ssm_norm[0]),
        w_a[0].astype(BF16), w_b[0].astype(BF16), w_o[0].astype(BF16))
    (out,) = _ffn_call(h2.reshape(b * l, d), row(ffn2_norm[0]), ffn2_w_in[0].astype(BF16),
                       ffn2_w_out[0].astype(BF16), row(final_norm), emit_h=False, norm_dtype=F32)
    return out.reshape(b, l, d)
```

```python
import jax
import jax.numpy as jnp
import numpy as np
from jax import lax
from jax.experimental import pallas as pl
from jax.experimental.pallas import tpu as pltpu

F32 = jnp.float32
BF16 = jnp.bfloat16

LANES = 128
VMEM_LIMIT_BYTES = 56 * 1024 * 1024

D_MODEL = 1024
D_FF = 4 * D_MODEL
EPS = 1e-6
SGU_BLOCK = 128
SGU_GROUPS = 8
SGU_CAUSAL = 64
SSM_INNER = 2 * D_MODEL
SSM_HEADDIM = 64
SSM_HEADS = SSM_INNER // SSM_HEADDIM
SSM_GROUPS = 8
SSM_HPG = SSM_HEADS // SSM_GROUPS
SSM_STATE = 128
SSM_CONV = 4
SSM_CONV_DIM = SSM_INNER + 2 * SSM_GROUPS * SSM_STATE
GROUP_W = SSM_HPG * SSM_HEADDIM

SSD_Q = 128

XBC0 = 0
DT0 = XBC0 + SSM_CONV_DIM
U0 = DT0 + LANES
V0 = U0 + D_MODEL
Z0 = V0 + D_MODEL
GA0 = Z0 + SSM_INNER
GB0 = GA0 + D_MODEL
P_DIM = GB0 + D_MODEL
CONV_BLOCK = D_MODEL
XBC_EDGES = tuple(range(XBC0, DT0 + 1, CONV_BLOCK))
P_EDGES = (DT0, Z0, GA0, P_DIM)

CONV_TAIL = 8

FFN_TM = 512
FFN_TF = 512
MIX_TL = 256


def _rms(x, g):
    ms = jnp.mean(x * x, axis=-1, keepdims=True)
    return x * lax.rsqrt(ms + EPS) * g


def _silu(x):
    h = 0.5 * x
    return h + h * jnp.tanh(h)


def _gelu_erf(x):
    return 0.5 * x * (1.0 + lax.erf(x * np.float32(1.0 / np.sqrt(2.0))))


def _dot(a, b):
    return jnp.dot(a, b, preferred_element_type=F32)


def _split3(x):
    hi = x.astype(BF16)
    r1 = x - hi.astype(F32)
    mid = r1.astype(BF16)
    lo = (r1 - mid.astype(F32)).astype(BF16)
    return hi, mid, lo


def _dot_exact_rhs(a_bf16_exact, x):
    hi, mid, lo = _split3(x)
    return _dot(a_bf16_exact, hi) + _dot(a_bf16_exact, mid) + _dot(a_bf16_exact, lo)


def _dot_exact_lhs(x, b_bf16_exact):
    hi, mid, lo = _split3(x)
    return _dot(hi, b_bf16_exact) + _dot(mid, b_bf16_exact) + _dot(lo, b_bf16_exact)


def _swiglu_half_step(x, g_ref, win_ref, wout_ref, acc_ref):
    n = _rms(x, g_ref[...]).astype(BF16)
    for c in range(D_FF // FFN_TF):
        lo, hi = c * FFN_TF, (c + 1) * FFN_TF
        gate = _dot(n, win_ref[:, lo:hi])
        up = _dot(n, win_ref[:, D_FF + lo:D_FF + hi])
        act = (_silu(gate) * up).astype(BF16)
        part = _dot(act, wout_ref[lo:hi, :])
        if c == 0:
            acc_ref[...] = part
        else:
            acc_ref[...] += part
    return x + 0.5 * acc_ref[...]


def _ffn1_kernel(x_ref, g_ref, win_ref, wout_ref, g2_ref, h_ref, n_ref, acc_ref):
    h = _swiglu_half_step(x_ref[...], g_ref, win_ref, wout_ref, acc_ref)
    h_ref[...] = h
    n_ref[...] = _rms(h, g2_ref[...]).astype(n_ref.dtype)


def _ffn2_kernel(h1_ref, m_ref, wo_ref, g_ref, win_ref, wout_ref, g2_ref, o_ref, acc_ref):
    x = h1_ref[...] + _dot(m_ref[...], wo_ref[...])
    h = _swiglu_half_step(x, g_ref, win_ref, wout_ref, acc_ref)
    o_ref[...] = _rms(h, g2_ref[...])


def _ffn_specs():
    const = lambda i: (0, 0)
    single = pl.Buffered(1)
    return dict(
        rows=pl.BlockSpec((FFN_TM, D_MODEL), lambda i: (i, 0)),
        vec=pl.BlockSpec((1, D_MODEL), const),
        w_in=pl.BlockSpec((D_MODEL, 2 * D_FF), const, pipeline_mode=single),
        w_out=pl.BlockSpec((D_FF, D_MODEL), const, pipeline_mode=single),
        w_sq=pl.BlockSpec((D_MODEL, D_MODEL), const, pipeline_mode=single),
        params=pltpu.CompilerParams(dimension_semantics=("parallel",), vmem_limit_bytes=VMEM_LIMIT_BYTES),
        scratch=[pltpu.VMEM((FFN_TM, D_MODEL), F32)],
    )


def _ffn1_call(x2, g, w_in, w_out, g2):
    t = x2.shape[0]
    s = _ffn_specs()
    return pl.pallas_call(
        _ffn1_kernel,
        grid=(t // FFN_TM,),
        in_specs=[s["rows"], s["vec"], s["w_in"], s["w_out"], s["vec"]],
        out_specs=[s["rows"], s["rows"]],
        out_shape=[jax.ShapeDtypeStruct((t, D_MODEL), F32), jax.ShapeDtypeStruct((t, D_MODEL), BF16)],
        scratch_shapes=s["scratch"],
        compiler_params=s["params"],
        name="ffn1",
    )(x2, g, w_in, w_out, g2)


def _ffn2_call(h1, merged, w_o, g, w_in, w_out, g2):
    t = h1.shape[0]
    s = _ffn_specs()
    return pl.pallas_call(
        _ffn2_kernel,
        grid=(t // FFN_TM,),
        in_specs=[s["rows"], s["rows"], s["w_sq"], s["vec"], s["w_in"], s["w_out"], s["vec"]],
        out_specs=s["rows"],
        out_shape=jax.ShapeDtypeStruct((t, D_MODEL), F32),
        scratch_shapes=s["scratch"],
        compiler_params=s["params"],
        name="ffn2",
    )(h1, merged, w_o, g, w_in, w_out, g2)


def _mixer_kernel(n_ref, win_ref, lng_ref, lnb_ref, ws_ref, bst_ref, cw_ref, cb_ref,
                  dtb_ref, alog_ref, dskip_ref, nrm_ref, wa_ref, wb_ref,
                  out_ref, state_ref, *bufs):
    tl = n_ref.shape[0]
    ext_refs = bufs[:len(XBC_EDGES) - 1]
    p_refs = dict(zip(P_EDGES[:-1], bufs[len(XBC_EDGES) - 1:]))

    @pl.when(pl.program_id(1) == 0)
    def _():
        state_ref[...] = jnp.zeros_like(state_ref)
        for ext_ref in ext_refs:
            ext_ref[0:CONV_TAIL, :] = jnp.zeros((CONV_TAIL, CONV_BLOCK), F32)

    def load(rows, c0, width):
        base = max(e for e in P_EDGES[:-1] if e <= c0)
        return p_refs[base][rows, c0 - base:c0 - base + width]

    def in_proj(c0, c1):
        res = _dot(n_ref[...], win_ref[:, c0:c1])
        if c1 <= DT0:
            ext_refs[c0 // CONV_BLOCK][CONV_TAIL:CONV_TAIL + tl, :] = res
        else:
            p_refs[c0][...] = res

    for c0, c1 in zip(XBC_EDGES[:-1], XBC_EDGES[1:]):
        in_proj(c0, c1)

    row = lax.broadcasted_iota(jnp.int32, (SSD_Q, SSD_Q), 0)
    col = lax.broadcasted_iota(jnp.int32, (SSD_Q, SSD_Q), 1)
    causal = row >= col
    tri_b = causal.astype(F32).astype(BF16)
    sgu_mask = ((col // SGU_CAUSAL) <= (row // SGU_CAUSAL)).astype(F32)
    e_row = lax.broadcasted_iota(jnp.int32, (LANES, SSM_INNER), 0)
    e_col = lax.broadcasted_iota(jnp.int32, (LANES, SSM_INNER), 1)
    expand_b = (e_col // SSM_HEADDIM == e_row).astype(F32).astype(BF16)
    even_half = col < SSM_HEADDIM
    a_neg = -jnp.exp(alog_ref[...])

    for r in range(tl // SSD_Q):
        r0 = r * SSD_Q
        rows = pl.ds(r0, SSD_Q)

        xbc_parts = []
        for c0 in range(0, SSM_CONV_DIM, D_MODEL):
            cols = slice(c0, c0 + D_MODEL)
            ext_ref = ext_refs[c0 // CONV_BLOCK]
            slab = ext_ref[pl.ds(r0, CONV_TAIL + SSD_Q), :]
            conv = cw_ref[SSM_CONV - 1:SSM_CONV, cols] * slab
            for lag in range(1, SSM_CONV):
                k = SSM_CONV - 1 - lag
                conv = conv + pltpu.roll(cw_ref[k:k + 1, cols] * slab, lag, axis=0)
            xbc_parts.append(_silu(conv[CONV_TAIL:, :] + cb_ref[:, cols]))
        xs = jnp.concatenate(xbc_parts[0:2], axis=1)
        bm = xbc_parts[2].astype(BF16)
        bm_t = [xbc_parts[2][:, g * SSM_STATE:(g + 1) * SSM_STATE].T.astype(BF16)
                for g in range(SSM_GROUPS)]
        cm = xbc_parts[3]
        cm_b = cm.astype(BF16)
        xs_b = xs.astype(BF16)

        if r == 0:
            in_proj(DT0, Z0)
        dt = jax.nn.softplus(load(rows, DT0, LANES) + dtb_ref[...])
        acs = _dot_exact_rhs(tri_b, dt * a_neg)
        acs_t = acs.T
        dt_t = dt.T
        ea = jnp.exp(acs)
        last = acs[SSD_Q - 1:SSD_Q, :]
        wgt = dt * jnp.exp(last - acs)
        cdec = jnp.broadcast_to(jnp.exp(last), (16, LANES))
        expd = _dot_exact_lhs(jnp.concatenate([wgt, cdec], axis=0), expand_b)
        xw = (xs * expd[0:SSD_Q, :]).astype(BF16)
        cdec_x = expd[SSD_Q:SSD_Q + 1, :]

        u = _gelu_erf(load(rows, U0, D_MODEL))
        v = _gelu_erf(load(rows, V0, D_MODEL))
        mu = jnp.mean(v, axis=-1, keepdims=True)
        vc = v - mu
        var = jnp.mean(vc * vc, axis=-1, keepdims=True)
        vln = (vc * lax.rsqrt(var + EPS) * lng_ref[...] + lnb_ref[...]).astype(BF16)
        f_parts = []
        for g in range(SGU_GROUPS):
            wm = (ws_ref[g] * sgu_mask).astype(BF16)
            fg = _dot(wm, vln[:, g * LANES:(g + 1) * LANES]) + bst_ref[:, g:g + 1]
            f_parts.append(fg)
        ya = (u * jnp.concatenate(f_parts, axis=1)).astype(BF16)

        if r == 0:
            in_proj(Z0, GA0)
        y_parts = []
        for g in range(SSM_GROUPS):
            if r == 0 and g == SSM_GROUPS // 2:
                in_proj(GA0, P_DIM)
            gs = slice(g * SSM_STATE, (g + 1) * SSM_STATE)
            cb = lax.dot_general(cm_b[:, gs], bm[:, gs], (((1,), (1,)), ((), ())),
                                 preferred_element_type=F32)
            st = state_ref[g]
            st_b = st.astype(BF16)
            cm_g = cm[:, gs]
            for j in range(SSM_HPG // 2):
                c0 = g * GROUP_W + j * LANES
                rhs = jnp.concatenate(
                    [xs_b[:, c0:c0 + LANES], st_b[:, j * LANES:(j + 1) * LANES]], axis=0)
                y_heads = []
                for k in (2 * j, 2 * j + 1):
                    h = g * SSM_HPG + k
                    sg = acs[:, h:h + 1] - acs_t[h:h + 1, :]
                    decay = jnp.exp(jnp.where(causal, sg, -jnp.inf))
                    scores = (cb * decay * dt_t[h:h + 1, :]).astype(BF16)
                    readout = (cm_g * ea[:, h:h + 1]).astype(BF16)
                    y_heads.append(_dot(jnp.concatenate([scores, readout], axis=1), rhs))
                y_parts.append(jnp.where(even_half, y_heads[0], y_heads[1]))
            new = _dot(bm_t[g], xw[:, g * GROUP_W:(g + 1) * GROUP_W])
            state_ref[g] = st * cdec_x[:, g * GROUP_W:(g + 1) * GROUP_W] + new
        y = jnp.concatenate(y_parts, axis=1) + xs * dskip_ref[...]

        yg = y * _silu(load(rows, Z0, SSM_INNER))
        n_parts = []
        for g in range(SSM_GROUPS):
            blk = yg[:, g * GROUP_W:(g + 1) * GROUP_W]
            ms = jnp.mean(blk * blk, axis=-1, keepdims=True)
            n_parts.append(blk * lax.rsqrt(ms + EPS))
        yb = (jnp.concatenate(n_parts, axis=1) * nrm_ref[...]).astype(BF16)

        merged = (jax.nn.sigmoid(load(rows, GA0, D_MODEL)) * _dot(ya, wa_ref[...])
                  + jax.nn.sigmoid(load(rows, GB0, D_MODEL)) * _dot(yb, wb_ref[...]))
        out_ref[rows, :] = merged.astype(BF16)

    for ext_ref in ext_refs:
        ext_ref[0:CONV_TAIL, :] = ext_ref[tl:tl + CONV_TAIL, :]


def _mixer_call(n3, w_cat, lng, lnb, w_s, b_st, conv_w, conv_b, dtb, alog, dskip, nrm, w_a, w_b):
    b, l, _ = n3.shape
    tl = min(MIX_TL, l)
    tile = lambda bi, i: (bi, i, 0)
    single = pl.Buffered(1)

    def resident(shape):
        return pl.BlockSpec(shape, lambda bi, i: (0,) * len(shape), pipeline_mode=single)

    return pl.pallas_call(
        _mixer_kernel,
        grid=(b, l // tl),
        in_specs=[
            pl.BlockSpec((None, tl, D_MODEL), tile),
            resident((D_MODEL, P_DIM)),
            resident((1, D_MODEL)),
            resident((1, D_MODEL)),
            resident((SGU_GROUPS, SGU_BLOCK, SGU_BLOCK)),
            resident((SGU_BLOCK, SGU_GROUPS)),
            resident((SSM_CONV, SSM_CONV_DIM)),
            resident((1, SSM_CONV_DIM)),
            resident((1, LANES)),
            resident((1, LANES)),
            resident((1, SSM_INNER)),
            resident((1, SSM_INNER)),
            resident((D_MODEL, D_MODEL)),
            resident((SSM_INNER, D_MODEL)),
        ],
        out_specs=pl.BlockSpec((None, tl, D_MODEL), tile),
        out_shape=jax.ShapeDtypeStruct((b, l, D_MODEL), BF16),
        scratch_shapes=[
            pltpu.VMEM((SSM_GROUPS, SSM_STATE, GROUP_W), F32),
            *[pltpu.VMEM((tl + CONV_TAIL, CONV_BLOCK), F32)
              for _ in XBC_EDGES[:-1]],
            *[pltpu.VMEM((tl, c1 - c0), F32)
              for c0, c1 in zip(P_EDGES[:-1], P_EDGES[1:])],
        ],
        compiler_params=pltpu.CompilerParams(
            dimension_semantics=("parallel", "arbitrary"), vmem_limit_bytes=VMEM_LIMIT_BYTES),
        name="mixer",
    )(n3, w_cat, lng, lnb, w_s, b_st, conv_w, conv_b, dtb, alog, dskip, nrm, w_a, w_b)


def _regroup_w_in(w):
    xbc0 = 2 * D_MODEL + SSM_INNER
    dt0 = xbc0 + SSM_CONV_DIM
    uvz, xbc, dt, gates = w[:, :xbc0], w[:, xbc0:dt0], w[:, dt0:dt0 + SSM_HEADS], w[:, dt0 + SSM_HEADS:]
    pad = jnp.zeros((w.shape[0], LANES - SSM_HEADS), w.dtype)
    return jnp.concatenate([xbc, dt, pad, uvz, gates], axis=1)


def _pad_lanes(v):
    return jnp.concatenate([v, jnp.zeros((LANES - v.shape[0],), v.dtype)]).reshape(1, LANES)


def kernel(x, ffn1_norm, ffn1_w_in, ffn1_w_out, mix_norm, w_in, sgu_ln_g, sgu_ln_b, sgu_w_s, sgu_b_s,
           conv_w, conv_b, dt_bias, a_log, d_skip, ssm_norm, w_a, w_b, w_o, ffn2_norm, ffn2_w_in,
           ffn2_w_out, final_norm):
    b, l, d = x.shape
    assert ffn1_norm.shape[0] == 1, "single-layer block"
    assert d == D_MODEL and l % SSD_Q == 0 and (b * l) % FFN_TM == 0
    row = lambda v: v.reshape(1, -1).astype(F32)
    bf = lambda w: w.astype(BF16)
    h1, n2 = _ffn1_call(x.reshape(b * l, d), row(ffn1_norm[0]), bf(ffn1_w_in[0]), bf(ffn1_w_out[0]),
                        row(mix_norm[0]))
    merged = _mixer_call(
        n2.reshape(b, l, d), bf(_regroup_w_in(w_in[0])),
        row(sgu_ln_g[0]), row(sgu_ln_b[0]), sgu_w_s[0], sgu_b_s[0].T,
        conv_w[0], row(conv_b[0]), _pad_lanes(dt_bias[0]), _pad_lanes(a_log[0]),
        row(jnp.repeat(d_skip[0], SSM_HEADDIM)), row(ssm_norm[0]), bf(w_a[0]), bf(w_b[0]))
    out = _ffn2_call(h1, merged.reshape(b * l, d), bf(w_o[0]), row(ffn2_norm[0]), bf(ffn2_w_in[0]),
                     bf(ffn2_w_out[0]), row(final_norm))
    return out.reshape(b, l, d)
```

```python
import jax
import jax.numpy as jnp
import numpy as np
from jax import lax
from jax.experimental import pallas as pl
from jax.experimental.pallas import tpu as pltpu

F32 = jnp.float32
BF16 = jnp.bfloat16

LANES = 128
VMEM_LIMIT_BYTES = 62 * 1024 * 1024

D_MODEL = 1024
D_FF = 4 * D_MODEL
EPS = 1e-6
SGU_BLOCK = 128
SGU_GROUPS = 8
SGU_CAUSAL = 64
SSM_INNER = 2 * D_MODEL
SSM_HEADDIM = 64
SSM_HEADS = SSM_INNER // SSM_HEADDIM
SSM_GROUPS = 8
SSM_HPG = SSM_HEADS // SSM_GROUPS
SSM_STATE = 128
SSM_CONV = 4
SSM_CONV_DIM = SSM_INNER + 2 * SSM_GROUPS * SSM_STATE
GROUP_W = SSM_HPG * SSM_HEADDIM

SSD_Q = 128

XBC0 = 0
DT0 = XBC0 + SSM_CONV_DIM
U0 = DT0 + LANES
V0 = U0 + D_MODEL
Z0 = V0 + D_MODEL
GA0 = Z0 + SSM_INNER
GB0 = GA0 + D_MODEL
P_DIM = GB0 + D_MODEL
CONV_BLOCK = D_MODEL
XBC_EDGES = tuple(range(XBC0, DT0 + 1, CONV_BLOCK))
P_EDGES = (DT0, Z0, GA0, P_DIM)

CONV_TAIL = 8

FFN_TM = 512
FFN_TF = 512
MIX_TL = 512


def _rms(x, g):
    ms = jnp.mean(x * x, axis=-1, keepdims=True)
    return x * lax.rsqrt(ms + EPS) * g


def _silu(x):
    h = 0.5 * x
    return h + h * jnp.tanh(h)


def _gelu_erf(x):
    return 0.5 * x * (1.0 + lax.erf(x * np.float32(1.0 / np.sqrt(2.0))))


def _dot(a, b):
    return jnp.dot(a, b, preferred_element_type=F32)


def _split3(x):
    hi = x.astype(BF16)
    r1 = x - hi.astype(F32)
    mid = r1.astype(BF16)
    lo = (r1 - mid.astype(F32)).astype(BF16)
    return hi, mid, lo


def _dot_exact_rhs(a_bf16_exact, x):
    hi, mid, lo = _split3(x)
    return _dot(a_bf16_exact, hi) + _dot(a_bf16_exact, mid) + _dot(a_bf16_exact, lo)


def _dot_exact_lhs(x, b_bf16_exact):
    hi, mid, lo = _split3(x)
    return _dot(hi, b_bf16_exact) + _dot(mid, b_bf16_exact) + _dot(lo, b_bf16_exact)


def _swiglu_half_step(x, g_ref, win_ref, wout_ref, acc_ref):
    n = _rms(x, g_ref[...]).astype(BF16)
    for c in range(D_FF // FFN_TF):
        lo, hi = c * FFN_TF, (c + 1) * FFN_TF
        gate = _dot(n, win_ref[:, lo:hi])
        up = _dot(n, win_ref[:, D_FF + lo:D_FF + hi])
        act = (_silu(gate) * up).astype(BF16)
        part = _dot(act, wout_ref[lo:hi, :])
        if c == 0:
            acc_ref[...] = part
        else:
            acc_ref[...] += part
    return x + 0.5 * acc_ref[...]


def _ffn1_kernel(x_ref, g_ref, win_ref, wout_ref, g2_ref, h_ref, n_ref, acc_ref):
    h = _swiglu_half_step(x_ref[...], g_ref, win_ref, wout_ref, acc_ref)
    h_ref[...] = h
    n_ref[...] = _rms(h, g2_ref[...]).astype(n_ref.dtype)


def _ffn2_kernel(h1_ref, m_ref, wo_ref, g_ref, win_ref, wout_ref, g2_ref, o_ref, acc_ref):
    x = h1_ref[...] + _dot(m_ref[...], wo_ref[...])
    h = _swiglu_half_step(x, g_ref, win_ref, wout_ref, acc_ref)
    o_ref[...] = _rms(h, g2_ref[...])


def _ffn_specs():
    const = lambda i: (0, 0)
    single = pl.Buffered(1)
    return dict(
        rows=pl.BlockSpec((FFN_TM, D_MODEL), lambda i: (i, 0)),
        vec=pl.BlockSpec((1, D_MODEL), const),
        w_in=pl.BlockSpec((D_MODEL, 2 * D_FF), const, pipeline_mode=single),
        w_out=pl.BlockSpec((D_FF, D_MODEL), const, pipeline_mode=single),
        w_sq=pl.BlockSpec((D_MODEL, D_MODEL), const, pipeline_mode=single),
        params=pltpu.CompilerParams(dimension_semantics=("parallel",), vmem_limit_bytes=VMEM_LIMIT_BYTES),
        scratch=[pltpu.VMEM((FFN_TM, D_MODEL), F32)],
    )


def _ffn1_call(x2, g, w_in, w_out, g2):
    t = x2.shape[0]
    s = _ffn_specs()
    return pl.pallas_call(
        _ffn1_kernel,
        grid=(t // FFN_TM,),
        in_specs=[s["rows"], s["vec"], s["w_in"], s["w_out"], s["vec"]],
        out_specs=[s["rows"], s["rows"]],
        out_shape=[jax.ShapeDtypeStruct((t, D_MODEL), F32), jax.ShapeDtypeStruct((t, D_MODEL), BF16)],
        scratch_shapes=s["scratch"],
        compiler_params=s["params"],
        name="ffn1",
    )(x2, g, w_in, w_out, g2)


def _ffn2_call(h1, merged, w_o, g, w_in, w_out, g2):
    t = h1.shape[0]
    s = _ffn_specs()
    return pl.pallas_call(
        _ffn2_kernel,
        grid=(t // FFN_TM,),
        in_specs=[s["rows"], s["rows"], s["w_sq"], s["vec"], s["w_in"], s["w_out"], s["vec"]],
        out_specs=s["rows"],
        out_shape=jax.ShapeDtypeStruct((t, D_MODEL), F32),
        scratch_shapes=s["scratch"],
        compiler_params=s["params"],
        name="ffn2",
    )(h1, merged, w_o, g, w_in, w_out, g2)


def _mixer_kernel(n_ref, win_ref, lng_ref, lnb_ref, ws_ref, bst_ref, cw_ref, cb_ref,
                  dtb_ref, alog_ref, dskip_ref, nrm_ref, wa_ref, wb_ref,
                  out_ref, state_ref, *bufs):
    tl = n_ref.shape[0]
    ext_refs = bufs[:len(XBC_EDGES) - 1]
    p_refs = dict(zip(P_EDGES[:-1], bufs[len(XBC_EDGES) - 1:]))

    @pl.when(pl.program_id(1) == 0)
    def _():
        state_ref[...] = jnp.zeros_like(state_ref)
        for ext_ref in ext_refs:
            ext_ref[0:CONV_TAIL, :] = jnp.zeros((CONV_TAIL, CONV_BLOCK), F32)

    def load(rows, c0, width):
        base = max(e for e in P_EDGES[:-1] if e <= c0)
        return p_refs[base][rows, c0 - base:c0 - base + width]

    def in_proj(c0, c1):
        res = _dot(n_ref[...], win_ref[:, c0:c1])
        if c1 <= DT0:
            ext_refs[c0 // CONV_BLOCK][CONV_TAIL:CONV_TAIL + tl, :] = res
        else:
            p_refs[c0][...] = res

    for c0, c1 in zip(XBC_EDGES[:-1], XBC_EDGES[1:]):
        in_proj(c0, c1)

    row = lax.broadcasted_iota(jnp.int32, (SSD_Q, SSD_Q), 0)
    col = lax.broadcasted_iota(jnp.int32, (SSD_Q, SSD_Q), 1)
    causal = row >= col
    tri_b = causal.astype(F32).astype(BF16)
    sgu_mask = ((col // SGU_CAUSAL) <= (row // SGU_CAUSAL)).astype(F32)
    e_row = lax.broadcasted_iota(jnp.int32, (LANES, SSM_INNER), 0)
    e_col = lax.broadcasted_iota(jnp.int32, (LANES, SSM_INNER), 1)
    expand_b = (e_col // SSM_HEADDIM == e_row).astype(F32).astype(BF16)
    even_half = col < SSM_HEADDIM
    a_neg = -jnp.exp(alog_ref[...])

    for r in range(tl // SSD_Q):
        r0 = r * SSD_Q
        rows = pl.ds(r0, SSD_Q)

        xbc_parts = []
        for c0 in range(0, SSM_CONV_DIM, D_MODEL):
            cols = slice(c0, c0 + D_MODEL)
            ext_ref = ext_refs[c0 // CONV_BLOCK]
            slab = ext_ref[pl.ds(r0, CONV_TAIL + SSD_Q), :]
            conv = cw_ref[SSM_CONV - 1:SSM_CONV, cols] * slab
            for lag in range(1, SSM_CONV):
                k = SSM_CONV - 1 - lag
                conv = conv + pltpu.roll(cw_ref[k:k + 1, cols] * slab, lag, axis=0)
            xbc_parts.append(_silu(conv[CONV_TAIL:, :] + cb_ref[:, cols]))
        xs = jnp.concatenate(xbc_parts[0:2], axis=1)
        bm = xbc_parts[2].astype(BF16)
        bm_t = [xbc_parts[2][:, g * SSM_STATE:(g + 1) * SSM_STATE].T.astype(BF16)
                for g in range(SSM_GROUPS)]
        cm = xbc_parts[3]
        cm_b = cm.astype(BF16)
        xs_b = xs.astype(BF16)

        if r == 0:
            in_proj(DT0, Z0)
        dt = jax.nn.softplus(load(rows, DT0, LANES) + dtb_ref[...])
        acs = _dot_exact_rhs(tri_b, dt * a_neg)
        acs_t = acs.T
        dt_t = dt.T
        ea = jnp.exp(acs)
        last = acs[SSD_Q - 1:SSD_Q, :]
        wgt = dt * jnp.exp(last - acs)
        cdec = jnp.broadcast_to(jnp.exp(last), (16, LANES))
        expd = _dot_exact_lhs(jnp.concatenate([wgt, cdec], axis=0), expand_b)
        xw = (xs * expd[0:SSD_Q, :]).astype(BF16)
        cdec_x = expd[SSD_Q:SSD_Q + 1, :]

        u = _gelu_erf(load(rows, U0, D_MODEL))
        v = _gelu_erf(load(rows, V0, D_MODEL))
        mu = jnp.mean(v, axis=-1, keepdims=True)
        vc = v - mu
        var = jnp.mean(vc * vc, axis=-1, keepdims=True)
        vln = (vc * lax.rsqrt(var + EPS) * lng_ref[...] + lnb_ref[...]).astype(BF16)
        f_parts = []
        for g in range(SGU_GROUPS):
            wm = (ws_ref[g] * sgu_mask).astype(BF16)
            fg = _dot(wm, vln[:, g * LANES:(g + 1) * LANES]) + bst_ref[:, g:g + 1]
            f_parts.append(fg)
        ya = (u * jnp.concatenate(f_parts, axis=1)).astype(BF16)

        if r == 0:
            in_proj(Z0, GA0)
        y_parts = []
        for g in range(SSM_GROUPS):
            if r == 0 and g == SSM_GROUPS // 2:
                in_proj(GA0, P_DIM)
            gs = slice(g * SSM_STATE, (g + 1) * SSM_STATE)
            cb = lax.dot_general(cm_b[:, gs], bm[:, gs], (((1,), (1,)), ((), ())),
                                 preferred_element_type=F32)
            st = state_ref[g]
            st_b = st.astype(BF16)
            cm_g = cm[:, gs]
            for j in range(SSM_HPG // 2):
                c0 = g * GROUP_W + j * LANES
                rhs = jnp.concatenate(
                    [xs_b[:, c0:c0 + LANES], st_b[:, j * LANES:(j + 1) * LANES]], axis=0)
                y_heads = []
                for k in (2 * j, 2 * j + 1):
                    h = g * SSM_HPG + k
                    sg = acs[:, h:h + 1] - acs_t[h:h + 1, :]
                    decay = jnp.exp(jnp.where(causal, sg, -jnp.inf))
                    scores = (cb * decay * dt_t[h:h + 1, :]).astype(BF16)
                    readout = (cm_g * ea[:, h:h + 1]).astype(BF16)
                    y_heads.append(_dot(jnp.concatenate([scores, readout], axis=1), rhs))
                y_parts.append(jnp.where(even_half, y_heads[0], y_heads[1]))
            new = _dot(bm_t[g], xw[:, g * GROUP_W:(g + 1) * GROUP_W])
            state_ref[g] = st * cdec_x[:, g * GROUP_W:(g + 1) * GROUP_W] + new
        y = jnp.concatenate(y_parts, axis=1) + xs * dskip_ref[...]

        yg = y * _silu(load(rows, Z0, SSM_INNER))
        n_parts = []
        for g in range(SSM_GROUPS):
            blk = yg[:, g * GROUP_W:(g + 1) * GROUP_W]
            ms = jnp.mean(blk * blk, axis=-1, keepdims=True)
            n_parts.append(blk * lax.rsqrt(ms + EPS))
        yb = (jnp.concatenate(n_parts, axis=1) * nrm_ref[...]).astype(BF16)

        merged = (jax.nn.sigmoid(load(rows, GA0, D_MODEL)) * _dot(ya, wa_ref[...])
                  + jax.nn.sigmoid(load(rows, GB0, D_MODEL)) * _dot(yb, wb_ref[...]))
        out_ref[rows, :] = merged.astype(BF16)

    for ext_ref in ext_refs:
        ext_ref[0:CONV_TAIL, :] = ext_ref[tl:tl + CONV_TAIL, :]


def _mixer_call(n3, w_cat, lng, lnb, w_s, b_st, conv_w, conv_b, dtb, alog, dskip, nrm, w_a, w_b):
    b, l, _ = n3.shape
    tl = min(MIX_TL, l)
    tile = lambda bi, i: (bi, i, 0)
    single = pl.Buffered(1)

    def resident(shape):
        return pl.BlockSpec(shape, lambda bi, i: (0,) * len(shape), pipeline_mode=single)

    return pl.pallas_call(
        _mixer_kernel,
        grid=(b, l // tl),
        in_specs=[
            pl.BlockSpec((None, tl, D_MODEL), tile),
            resident((D_MODEL, P_DIM)),
            resident((1, D_MODEL)),
            resident((1, D_MODEL)),
            resident((SGU_GROUPS, SGU_BLOCK, SGU_BLOCK)),
            resident((SGU_BLOCK, SGU_GROUPS)),
            resident((SSM_CONV, SSM_CONV_DIM)),
            resident((1, SSM_CONV_DIM)),
            resident((1, LANES)),
            resident((1, LANES)),
            resident((1, SSM_INNER)),
            resident((1, SSM_INNER)),
            resident((D_MODEL, D_MODEL)),
            resident((SSM_INNER, D_MODEL)),
        ],
        out_specs=pl.BlockSpec((None, tl, D_MODEL), tile),
        out_shape=jax.ShapeDtypeStruct((b, l, D_MODEL), BF16),
        scratch_shapes=[
            pltpu.VMEM((SSM_GROUPS, SSM_STATE, GROUP_W), F32),
            *[pltpu.VMEM((tl + CONV_TAIL, CONV_BLOCK), F32)
              for _ in XBC_EDGES[:-1]],
            *[pltpu.VMEM((tl, c1 - c0), F32)
              for c0, c1 in zip(P_EDGES[:-1], P_EDGES[1:])],
        ],
        compiler_params=pltpu.CompilerParams(
            dimension_semantics=("parallel", "arbitrary"), vmem_limit_bytes=VMEM_LIMIT_BYTES),
        name="mixer",
    )(n3, w_cat, lng, lnb, w_s, b_st, conv_w, conv_b, dtb, alog, dskip, nrm, w_a, w_b)


def _regroup_w_in(w):
    xbc0 = 2 * D_MODEL + SSM_INNER
    dt0 = xbc0 + SSM_CONV_DIM
    uvz, xbc, dt, gates = w[:, :xbc0], w[:, xbc0:dt0], w[:, dt0:dt0 + SSM_HEADS], w[:, dt0 + SSM_HEADS:]
    pad = jnp.zeros((w.shape[0], LANES - SSM_HEADS), w.dtype)
    return jnp.concatenate([xbc, dt, pad, uvz, gates], axis=1)


def _pad_lanes(v):
    return jnp.concatenate([v, jnp.zeros((LANES - v.shape[0],), v.dtype)]).reshape(1, LANES)


def kernel(x, ffn1_norm, ffn1_w_in, ffn1_w_out, mix_norm, w_in, sgu_ln_g, sgu_ln_b, sgu_w_s, sgu_b_s,
           conv_w, conv_b, dt_bias, a_log, d_skip, ssm_norm, w_a, w_b, w_o, ffn2_norm, ffn2_w_in,
           ffn2_w_out, final_norm):
    b, l, d = x.shape
    assert ffn1_norm.shape[0] == 1, "single-layer block"
    assert d == D_MODEL and l % SSD_Q == 0 and (b * l) % FFN_TM == 0
    row = lambda v: v.reshape(1, -1).astype(F32)
    bf = lambda w: w.astype(BF16)
    h1, n2 = _ffn1_call(x.reshape(b * l, d), row(ffn1_norm[0]), bf(ffn1_w_in[0]), bf(ffn1_w_out[0]),
                        row(mix_norm[0]))
    merged = _mixer_call(
        n2.reshape(b, l, d), _regroup_w_in(bf(w_in[0])),
        row(sgu_ln_g[0]), row(sgu_ln_b[0]), sgu_w_s[0], sgu_b_s[0].T,
        conv_w[0], row(conv_b[0]), _pad_lanes(dt_bias[0]), _pad_lanes(a_log[0]),
        row(jnp.repeat(d_skip[0], SSM_HEADDIM)), row(ssm_norm[0]), bf(w_a[0]), bf(w_b[0]))
    out = _ffn2_call(h1, merged.reshape(b * l, d), bf(w_o[0]), row(ffn2_norm[0]), bf(ffn2_w_in[0]),
                     bf(ffn2_w_out[0]), row(final_norm))
    return out.reshape(b, l, d)
```

```python
import jax
import jax.numpy as jnp
import numpy as np
from jax import lax
from jax.experimental import pallas as pl
from jax.experimental.pallas import tpu as pltpu

F32 = jnp.float32
BF16 = jnp.bfloat16

LANES = 128
VMEM_LIMIT_BYTES = 62 * 1024 * 1024

D_MODEL = 1024
D_FF = 4 * D_MODEL
EPS = 1e-6
SGU_BLOCK = 128
SGU_GROUPS = 8
SGU_CAUSAL = 64
SSM_INNER = 2 * D_MODEL
SSM_HEADDIM = 64
SSM_HEADS = SSM_INNER // SSM_HEADDIM
SSM_GROUPS = 8
SSM_HPG = SSM_HEADS // SSM_GROUPS
SSM_STATE = 128
SSM_CONV = 4
SSM_CONV_DIM = SSM_INNER + 2 * SSM_GROUPS * SSM_STATE
GROUP_W = SSM_HPG * SSM_HEADDIM

SSD_Q = 128

U0 = 0
V0 = U0 + D_MODEL
Z0 = V0 + D_MODEL
XBC0 = Z0 + SSM_INNER
MAIN_W = XBC0 + SSM_CONV_DIM
TAIL_GATES = 0
TAIL_DT = 2 * D_MODEL
TAIL_W = TAIL_DT + LANES
CONV_BLOCK = D_MODEL
N_CONV_BLOCKS = SSM_CONV_DIM // CONV_BLOCK
SEGMENTS = {"uv": ("main", U0, 2 * D_MODEL), "z": ("main", Z0, SSM_INNER),
            "gates": ("tail", TAIL_GATES, 2 * D_MODEL), "dt": ("tail", TAIL_DT, LANES)}

CONV_TAIL = 8

FFN_TM = 512
FFN_TF = 512
MIX_TL = 512


def _rms(x, g):
    ms = jnp.mean(x * x, axis=-1, keepdims=True)
    return x * lax.rsqrt(ms + EPS) * g


def _silu(x):
    h = 0.5 * x
    return h + h * jnp.tanh(h)


def _gelu_erf(x):
    return 0.5 * x * (1.0 + lax.erf(x * np.float32(1.0 / np.sqrt(2.0))))


def _dot(a, b):
    return jnp.dot(a, b, preferred_element_type=F32)


def _split3(x):
    hi = x.astype(BF16)
    r1 = x - hi.astype(F32)
    mid = r1.astype(BF16)
    lo = (r1 - mid.astype(F32)).astype(BF16)
    return hi, mid, lo


def _dot_exact_rhs(a_bf16_exact, x):
    hi, mid, lo = _split3(x)
    return _dot(a_bf16_exact, hi) + _dot(a_bf16_exact, mid) + _dot(a_bf16_exact, lo)


def _dot_exact_lhs(x, b_bf16_exact):
    hi, mid, lo = _split3(x)
    return _dot(hi, b_bf16_exact) + _dot(mid, b_bf16_exact) + _dot(lo, b_bf16_exact)


def _swiglu_half_step(x, g_ref, win_ref, wout_ref, acc_ref):
    n = _rms(x, g_ref[...]).astype(BF16)
    for c in range(D_FF // FFN_TF):
        lo, hi = c * FFN_TF, (c + 1) * FFN_TF
        gate = _dot(n, win_ref[:, lo:hi])
        up = _dot(n, win_ref[:, D_FF + lo:D_FF + hi])
        act = (_silu(gate) * up).astype(BF16)
        part = _dot(act, wout_ref[lo:hi, :])
        if c == 0:
            acc_ref[...] = part
        else:
            acc_ref[...] += part
    return x + 0.5 * acc_ref[...]


def _ffn1_kernel(x_ref, g_ref, win_ref, wout_ref, g2_ref, h_ref, n_ref, acc_ref):
    h = _swiglu_half_step(x_ref[...], g_ref, win_ref, wout_ref, acc_ref)
    h_ref[...] = h
    n_ref[...] = _rms(h, g2_ref[...]).astype(n_ref.dtype)


def _ffn2_kernel(h1_ref, m_ref, wo_ref, g_ref, win_ref, wout_ref, g2_ref, o_ref, acc_ref):
    x = h1_ref[...] + _dot(m_ref[...], wo_ref[...])
    h = _swiglu_half_step(x, g_ref, win_ref, wout_ref, acc_ref)
    o_ref[...] = _rms(h, g2_ref[...])


def _ffn_specs():
    const = lambda i: (0, 0)
    single = pl.Buffered(1)
    return dict(
        rows=pl.BlockSpec((FFN_TM, D_MODEL), lambda i: (i, 0)),
        vec=pl.BlockSpec((1, D_MODEL), const),
        w_in=pl.BlockSpec((D_MODEL, 2 * D_FF), const, pipeline_mode=single),
        w_out=pl.BlockSpec((D_FF, D_MODEL), const, pipeline_mode=single),
        w_sq=pl.BlockSpec((D_MODEL, D_MODEL), const, pipeline_mode=single),
        params=pltpu.CompilerParams(dimension_semantics=("parallel",), vmem_limit_bytes=VMEM_LIMIT_BYTES),
        scratch=[pltpu.VMEM((FFN_TM, D_MODEL), F32)],
    )


def _ffn1_call(x2, g, w_in, w_out, g2):
    t = x2.shape[0]
    s = _ffn_specs()
    return pl.pallas_call(
        _ffn1_kernel,
        grid=(t // FFN_TM,),
        in_specs=[s["rows"], s["vec"], s["w_in"], s["w_out"], s["vec"]],
        out_specs=[s["rows"], s["rows"]],
        out_shape=[jax.ShapeDtypeStruct((t, D_MODEL), F32), jax.ShapeDtypeStruct((t, D_MODEL), BF16)],
        scratch_shapes=s["scratch"],
        compiler_params=s["params"],
        name="ffn1",
    )(x2, g, w_in, w_out, g2)


def _ffn2_call(h1, merged, w_o, g, w_in, w_out, g2):
    t = h1.shape[0]
    s = _ffn_specs()
    return pl.pallas_call(
        _ffn2_kernel,
        grid=(t // FFN_TM,),
        in_specs=[s["rows"], s["rows"], s["w_sq"], s["vec"], s["w_in"], s["w_out"], s["vec"]],
        out_specs=s["rows"],
        out_shape=jax.ShapeDtypeStruct((t, D_MODEL), F32),
        scratch_shapes=s["scratch"],
        compiler_params=s["params"],
        name="ffn2",
    )(h1, merged, w_o, g, w_in, w_out, g2)


def _mixer_kernel(n_ref, win_ref, wtail_ref, lng_ref, lnb_ref, ws_ref, bst_ref, cw_ref, cb_ref,
                  dtb_ref, alog_ref, dskip_ref, nrm_ref, wa_ref, wb_ref,
                  out_ref, state_ref, *bufs):
    tl = n_ref.shape[0]
    ext_refs = bufs[:N_CONV_BLOCKS]
    p_refs = dict(zip(SEGMENTS, bufs[N_CONV_BLOCKS:]))

    @pl.when(pl.program_id(1) == 0)
    def _():
        state_ref[...] = jnp.zeros_like(state_ref)
        for ext_ref in ext_refs:
            ext_ref[0:CONV_TAIL, :] = jnp.zeros((CONV_TAIL, CONV_BLOCK), F32)

    def load(rows, name, c0=0, width=None):
        return p_refs[name][rows, c0:c0 + (width or SEGMENTS[name][2])]

    def in_proj(name):
        which, c0, width = SEGMENTS[name]
        w_ref = win_ref if which == "main" else wtail_ref
        p_refs[name][...] = _dot(n_ref[...], w_ref[:, c0:c0 + width])

    for q, ext_ref in enumerate(ext_refs):
        c0 = XBC0 + q * CONV_BLOCK
        ext_ref[CONV_TAIL:CONV_TAIL + tl, :] = _dot(n_ref[...], win_ref[:, c0:c0 + CONV_BLOCK])

    row = lax.broadcasted_iota(jnp.int32, (SSD_Q, SSD_Q), 0)
    col = lax.broadcasted_iota(jnp.int32, (SSD_Q, SSD_Q), 1)
    causal = row >= col
    tri_b = causal.astype(F32).astype(BF16)
    sgu_mask = ((col // SGU_CAUSAL) <= (row // SGU_CAUSAL)).astype(F32)
    e_row = lax.broadcasted_iota(jnp.int32, (LANES, SSM_INNER), 0)
    e_col = lax.broadcasted_iota(jnp.int32, (LANES, SSM_INNER), 1)
    expand_b = (e_col // SSM_HEADDIM == e_row).astype(F32).astype(BF16)
    even_half = col < SSM_HEADDIM
    a_neg = -jnp.exp(alog_ref[...])

    for r in range(tl // SSD_Q):
        r0 = r * SSD_Q
        rows = pl.ds(r0, SSD_Q)

        xbc_parts = []
        for c0 in range(0, SSM_CONV_DIM, D_MODEL):
            cols = slice(c0, c0 + D_MODEL)
            ext_ref = ext_refs[c0 // CONV_BLOCK]
            slab = ext_ref[pl.ds(r0, CONV_TAIL + SSD_Q), :]
            conv = cw_ref[SSM_CONV - 1:SSM_CONV, cols] * slab
            for lag in range(1, SSM_CONV):
                k = SSM_CONV - 1 - lag
                conv = conv + pltpu.roll(cw_ref[k:k + 1, cols] * slab, lag, axis=0)
            xbc_parts.append(_silu(conv[CONV_TAIL:, :] + cb_ref[:, cols]))
        xs = jnp.concatenate(xbc_parts[0:2], axis=1)
        bm = xbc_parts[2].astype(BF16)
        bm_t = [xbc_parts[2][:, g * SSM_STATE:(g + 1) * SSM_STATE].T.astype(BF16)
                for g in range(SSM_GROUPS)]
        cm = xbc_parts[3]
        cm_b = cm.astype(BF16)
        xs_b = xs.astype(BF16)

        if r == 0:
            in_proj("dt")
            in_proj("uv")
        dt = jax.nn.softplus(load(rows, "dt") + dtb_ref[...])
        acs = _dot_exact_rhs(tri_b, dt * a_neg)
        acs_t = acs.T
        dt_t = dt.T
        ea = jnp.exp(acs)
        last = acs[SSD_Q - 1:SSD_Q, :]
        wgt = dt * jnp.exp(last - acs)
        cdec = jnp.broadcast_to(jnp.exp(last), (16, LANES))
        expd = _dot_exact_lhs(jnp.concatenate([wgt, cdec], axis=0), expand_b)
        xw = (xs * expd[0:SSD_Q, :]).astype(BF16)
        cdec_x = expd[SSD_Q:SSD_Q + 1, :]

        u = _gelu_erf(load(rows, "uv", 0, D_MODEL))
        v = _gelu_erf(load(rows, "uv", D_MODEL, D_MODEL))
        mu = jnp.mean(v, axis=-1, keepdims=True)
        vc = v - mu
        var = jnp.mean(vc * vc, axis=-1, keepdims=True)
        vln = (vc * lax.rsqrt(var + EPS) * lng_ref[...] + lnb_ref[...]).astype(BF16)
        f_parts = []
        for g in range(SGU_GROUPS):
            wm = (ws_ref[g] * sgu_mask).astype(BF16)
            fg = _dot(wm, vln[:, g * LANES:(g + 1) * LANES]) + bst_ref[:, g:g + 1]
            f_parts.append(fg)
        ya = (u * jnp.concatenate(f_parts, axis=1)).astype(BF16)

        if r == 0:
            in_proj("z")
        y_parts = []
        for g in range(SSM_GROUPS):
            if r == 0 and g == SSM_GROUPS // 2:
                in_proj("gates")
            gs = slice(g * SSM_STATE, (g + 1) * SSM_STATE)
            cb = lax.dot_general(cm_b[:, gs], bm[:, gs], (((1,), (1,)), ((), ())),
                                 preferred_element_type=F32)
            st = state_ref[g]
            st_b = st.astype(BF16)
            cm_g = cm[:, gs]
            for j in range(SSM_HPG // 2):
                c0 = g * GROUP_W + j * LANES
                rhs = jnp.concatenate(
                    [xs_b[:, c0:c0 + LANES], st_b[:, j * LANES:(j + 1) * LANES]], axis=0)
                y_heads = []
                for k in (2 * j, 2 * j + 1):
                    h = g * SSM_HPG + k
                    sg = acs[:, h:h + 1] - acs_t[h:h + 1, :]
                    decay = jnp.exp(jnp.where(causal, sg, -jnp.inf))
                    scores = (cb * decay * dt_t[h:h + 1, :]).astype(BF16)
                    readout = (cm_g * ea[:, h:h + 1]).astype(BF16)
                    y_heads.append(_dot(jnp.concatenate([scores, readout], axis=1), rhs))
                y_parts.append(jnp.where(even_half, y_heads[0], y_heads[1]))
            new = _dot(bm_t[g], xw[:, g * GROUP_W:(g + 1) * GROUP_W])
            state_ref[g] = st * cdec_x[:, g * GROUP_W:(g + 1) * GROUP_W] + new
        y = jnp.concatenate(y_parts, axis=1) + xs * dskip_ref[...]

        yg = y * _silu(load(rows, "z"))
        n_parts = []
        for g in range(SSM_GROUPS):
            blk = yg[:, g * GROUP_W:(g + 1) * GROUP_W]
            ms = jnp.mean(blk * blk, axis=-1, keepdims=True)
            n_parts.append(blk * lax.rsqrt(ms + EPS))
        yb = (jnp.concatenate(n_parts, axis=1) * nrm_ref[...]).astype(BF16)

        merged = (jax.nn.sigmoid(load(rows, "gates", 0, D_MODEL)) * _dot(ya, wa_ref[...])
                  + jax.nn.sigmoid(load(rows, "gates", D_MODEL, D_MODEL)) * _dot(yb, wb_ref[...]))
        out_ref[rows, :] = merged.astype(BF16)

    for ext_ref in ext_refs:
        ext_ref[0:CONV_TAIL, :] = ext_ref[tl:tl + CONV_TAIL, :]


def _mixer_call(n3, w_main, w_tail, lng, lnb, w_s, b_st, conv_w, conv_b, dtb, alog, dskip, nrm, w_a, w_b):
    b, l, _ = n3.shape
    tl = min(MIX_TL, l)
    tile = lambda bi, i: (bi, i, 0)
    single = pl.Buffered(1)

    def resident(shape):
        return pl.BlockSpec(shape, lambda bi, i: (0,) * len(shape), pipeline_mode=single)

    return pl.pallas_call(
        _mixer_kernel,
        grid=(b, l // tl),
        in_specs=[
            pl.BlockSpec((None, tl, D_MODEL), tile),
            pl.BlockSpec((D_MODEL, MAIN_W), lambda bi, i: (0, 0), pipeline_mode=single),
            resident((D_MODEL, TAIL_W)),
            resident((1, D_MODEL)),
            resident((1, D_MODEL)),
            resident((SGU_GROUPS, SGU_BLOCK, SGU_BLOCK)),
            resident((SGU_BLOCK, SGU_GROUPS)),
            resident((SSM_CONV, SSM_CONV_DIM)),
            resident((1, SSM_CONV_DIM)),
            resident((1, LANES)),
            resident((1, LANES)),
            resident((1, SSM_INNER)),
            resident((1, SSM_INNER)),
            resident((D_MODEL, D_MODEL)),
            resident((SSM_INNER, D_MODEL)),
        ],
        out_specs=pl.BlockSpec((None, tl, D_MODEL), tile),
        out_shape=jax.ShapeDtypeStruct((b, l, D_MODEL), BF16),
        scratch_shapes=[
            pltpu.VMEM((SSM_GROUPS, SSM_STATE, GROUP_W), F32),
            *[pltpu.VMEM((tl + CONV_TAIL, CONV_BLOCK), F32)
              for _ in range(N_CONV_BLOCKS)],
            *[pltpu.VMEM((tl, width), F32)
              for _, _, width in SEGMENTS.values()],
        ],
        compiler_params=pltpu.CompilerParams(
            dimension_semantics=("parallel", "arbitrary"), vmem_limit_bytes=VMEM_LIMIT_BYTES),
        name="mixer",
    )(n3, w_main, w_tail, lng, lnb, w_s, b_st, conv_w, conv_b, dtb, alog, dskip, nrm, w_a, w_b)


def _tail_w_in(w):
    dt0 = MAIN_W
    dt, gates = w[:, dt0:dt0 + SSM_HEADS], w[:, dt0 + SSM_HEADS:]
    pad = jnp.zeros((w.shape[0], LANES - SSM_HEADS), w.dtype)
    return jnp.concatenate([gates, dt, pad], axis=1)


def _pad_lanes(v):
    return jnp.concatenate([v, jnp.zeros((LANES - v.shape[0],), v.dtype)]).reshape(1, LANES)


def kernel(x, ffn1_norm, ffn1_w_in, ffn1_w_out, mix_norm, w_in, sgu_ln_g, sgu_ln_b, sgu_w_s, sgu_b_s,
           conv_w, conv_b, dt_bias, a_log, d_skip, ssm_norm, w_a, w_b, w_o, ffn2_norm, ffn2_w_in,
           ffn2_w_out, final_norm):
    b, l, d = x.shape
    assert ffn1_norm.shape[0] == 1, "single-layer block"
    assert d == D_MODEL and l % SSD_Q == 0 and (b * l) % FFN_TM == 0
    row = lambda v: v.reshape(1, -1).astype(F32)
    bf = lambda w: w.astype(BF16)
    h1, n2 = _ffn1_call(x.reshape(b * l, d), row(ffn1_norm[0]), bf(ffn1_w_in[0]), bf(ffn1_w_out[0]),
                        row(mix_norm[0]))
    w_in_b = bf(w_in[0])
    merged = _mixer_call(
        n2.reshape(b, l, d), w_in_b, _tail_w_in(w_in_b),
        row(sgu_ln_g[0]), row(sgu_ln_b[0]), sgu_w_s[0], sgu_b_s[0].T,
        conv_w[0], row(conv_b[0]), _pad_lanes(dt_bias[0]), _pad_lanes(a_log[0]),
        row(jnp.repeat(d_skip[0], SSM_HEADDIM)), row(ssm_norm[0]), bf(w_a[0]), bf(w_b[0]))
    out = _ffn2_call(h1, merged.reshape(b * l, d), bf(w_o[0]), row(ffn2_norm[0]), bf(ffn2_w_in[0]),
                     bf(ffn2_w_out[0]), row(final_norm))
    return out.reshape(b, l, d)
```

```python
import jax
import jax.numpy as jnp
import numpy as np
from jax import lax
from jax.experimental import pallas as pl
from jax.experimental.pallas import tpu as pltpu

F32 = jnp.float32
BF16 = jnp.bfloat16

LANES = 128
VMEM_LIMIT_BYTES = 62 * 1024 * 1024

D_MODEL = 1024
D_FF = 4 * D_MODEL
EPS = 1e-6
SGU_BLOCK = 128
SGU_GROUPS = 8
SGU_CAUSAL = 64
SSM_INNER = 2 * D_MODEL
SSM_HEADDIM = 64
SSM_HEADS = SSM_INNER // SSM_HEADDIM
SSM_GROUPS = 8
SSM_HPG = SSM_HEADS // SSM_GROUPS
SSM_STATE = 128
SSM_CONV = 4
SSM_CONV_DIM = SSM_INNER + 2 * SSM_GROUPS * SSM_STATE
GROUP_W = SSM_HPG * SSM_HEADDIM

SSD_Q = 128

U0 = 0
V0 = U0 + D_MODEL
Z0 = V0 + D_MODEL
XBC0 = Z0 + SSM_INNER
MAIN_W = XBC0 + SSM_CONV_DIM
TAIL_GATES = 0
TAIL_DT = 2 * D_MODEL
TAIL_W = TAIL_DT + LANES
CONV_BLOCK = D_MODEL
N_CONV_BLOCKS = SSM_CONV_DIM // CONV_BLOCK
SEGMENTS = {"uv": ("main", U0, 2 * D_MODEL), "z": ("main", Z0, SSM_INNER),
            "gates": ("tail", TAIL_GATES, 2 * D_MODEL), "dt": ("tail", TAIL_DT, LANES)}

CONV_TAIL = 8

FFN_TM = 512
FFN_TF = 512
MIX_TL = 512
IN_ROWS = 256


def _rms(x, g):
    ms = jnp.mean(x * x, axis=-1, keepdims=True)
    return x * lax.rsqrt(ms + EPS) * g


def _silu_of_twice(h):
    return h + h * jnp.tanh(h)


def _silu(x):
    return _silu_of_twice(0.5 * x)


def _gelu_erf(x):
    return 0.5 * x * (1.0 + lax.erf(x * np.float32(1.0 / np.sqrt(2.0))))


def _dot(a, b):
    return jnp.dot(a, b, preferred_element_type=F32)


def _split3(x):
    hi = x.astype(BF16)
    r1 = x - hi.astype(F32)
    mid = r1.astype(BF16)
    lo = (r1 - mid.astype(F32)).astype(BF16)
    return hi, mid, lo


def _dot_exact_rhs(a_bf16_exact, x):
    hi, mid, lo = _split3(x)
    return _dot(a_bf16_exact, hi) + _dot(a_bf16_exact, mid) + _dot(a_bf16_exact, lo)


def _dot_exact_lhs(x, b_bf16_exact):
    hi, mid, lo = _split3(x)
    return _dot(hi, b_bf16_exact) + _dot(mid, b_bf16_exact) + _dot(lo, b_bf16_exact)


def _swiglu_half_step(x, g_ref, win_ref, wout_ref, acc_ref):
    n = _rms(x, g_ref[...]).astype(BF16)
    for c in range(D_FF // FFN_TF):
        lo, hi = c * FFN_TF, (c + 1) * FFN_TF
        gate = _dot(n, win_ref[:, lo:hi])
        up = _dot(n, win_ref[:, D_FF + lo:D_FF + hi])
        act = (_silu(gate) * up).astype(BF16)
        part = _dot(act, wout_ref[lo:hi, :])
        if c == 0:
            acc_ref[...] = part
        else:
            acc_ref[...] += part
    return x + 0.5 * acc_ref[...]


def _ffn1_kernel(x_ref, g_ref, win_ref, wout_ref, g2_ref, h_ref, n_ref, acc_ref):
    h = _swiglu_half_step(x_ref[...], g_ref, win_ref, wout_ref, acc_ref)
    h_ref[...] = h
    n_ref[...] = _rms(h, g2_ref[...]).astype(n_ref.dtype)


def _ffn2_kernel(h1_ref, m_ref, wo_ref, g_ref, win_ref, wout_ref, g2_ref, o_ref, acc_ref):
    x = h1_ref[...] + _dot(m_ref[...], wo_ref[...])
    h = _swiglu_half_step(x, g_ref, win_ref, wout_ref, acc_ref)
    o_ref[...] = _rms(h, g2_ref[...])


def _ffn_specs():
    const = lambda i: (0, 0)
    single = pl.Buffered(1)
    return dict(
        rows=pl.BlockSpec((FFN_TM, D_MODEL), lambda i: (i, 0)),
        vec=pl.BlockSpec((1, D_MODEL), const),
        w_in=pl.BlockSpec((D_MODEL, 2 * D_FF), const, pipeline_mode=single),
        w_out=pl.BlockSpec((D_FF, D_MODEL), const, pipeline_mode=single),
        w_sq=pl.BlockSpec((D_MODEL, D_MODEL), const, pipeline_mode=single),
        params=pltpu.CompilerParams(dimension_semantics=("parallel",), vmem_limit_bytes=VMEM_LIMIT_BYTES),
        scratch=[pltpu.VMEM((FFN_TM, D_MODEL), F32)],
    )


def _ffn1_call(x2, g, w_in, w_out, g2):
    t = x2.shape[0]
    s = _ffn_specs()
    return pl.pallas_call(
        _ffn1_kernel,
        grid=(t // FFN_TM,),
        in_specs=[s["rows"], s["vec"], s["w_in"], s["w_out"], s["vec"]],
        out_specs=[s["rows"], s["rows"]],
        out_shape=[jax.ShapeDtypeStruct((t, D_MODEL), F32), jax.ShapeDtypeStruct((t, D_MODEL), BF16)],
        scratch_shapes=s["scratch"],
        compiler_params=s["params"],
        name="ffn1",
    )(x2, g, w_in, w_out, g2)


def _ffn2_call(h1, merged, w_o, g, w_in, w_out, g2):
    t = h1.shape[0]
    s = _ffn_specs()
    return pl.pallas_call(
        _ffn2_kernel,
        grid=(t // FFN_TM,),
        in_specs=[s["rows"], s["rows"], s["w_sq"], s["vec"], s["w_in"], s["w_out"], s["vec"]],
        out_specs=s["rows"],
        out_shape=jax.ShapeDtypeStruct((t, D_MODEL), F32),
        scratch_shapes=s["scratch"],
        compiler_params=s["params"],
        name="ffn2",
    )(h1, merged, w_o, g, w_in, w_out, g2)


def _mixer_kernel(n_ref, win_ref, wtail_ref, lng_ref, lnb_ref, ws_ref, bst_ref, cw_ref, cb_ref,
                  dtb_ref, alog_ref, dskip_ref, nrm_ref, wa_ref, wb_ref,
                  out_ref, state_ref, *bufs):
    tl = n_ref.shape[0]
    ext_refs = bufs[:N_CONV_BLOCKS]
    p_refs = dict(zip(SEGMENTS, bufs[N_CONV_BLOCKS:]))

    @pl.when(pl.program_id(1) == 0)
    def _():
        state_ref[...] = jnp.zeros_like(state_ref)
        for ext_ref in ext_refs:
            ext_ref[0:CONV_TAIL, :] = jnp.zeros((CONV_TAIL, CONV_BLOCK), F32)

    def load(rows, name, c0=0, width=None):
        return p_refs[name][rows, c0:c0 + (width or SEGMENTS[name][2])]

    def in_proj(name, part=0):
        rows = pl.ds(part * IN_ROWS, IN_ROWS)
        if isinstance(name, int):
            c0 = XBC0 + name * CONV_BLOCK
            ext_refs[name][pl.ds(CONV_TAIL + part * IN_ROWS, IN_ROWS), :] = _dot(
                n_ref[rows, :], win_ref[:, c0:c0 + CONV_BLOCK])
        else:
            which, c0, width = SEGMENTS[name]
            w_ref = win_ref if which == "main" else wtail_ref
            p_refs[name][rows, :] = _dot(n_ref[rows, :], w_ref[:, c0:c0 + width])

    products = (*range(N_CONV_BLOCKS), "dt", "uv", "z", "gates")
    later = [(name, part) for part in range(1, tl // IN_ROWS) for name in products]
    issued = [0]

    def next_in_proj():
        if issued[0] < len(later):
            in_proj(*later[issued[0]])
            issued[0] += 1

    def finish_in_proj(part):
        while issued[0] < min(part * len(products), len(later)):
            next_in_proj()

    for q in range(N_CONV_BLOCKS):
        in_proj(q)

    row = lax.broadcasted_iota(jnp.int32, (SSD_Q, SSD_Q), 0)
    col = lax.broadcasted_iota(jnp.int32, (SSD_Q, SSD_Q), 1)
    causal = row >= col
    tri_b = causal.astype(F32).astype(BF16)
    sgu_mask = ((col // SGU_CAUSAL) <= (row // SGU_CAUSAL)).astype(F32)
    e_row = lax.broadcasted_iota(jnp.int32, (LANES, SSM_INNER), 0)
    e_col = lax.broadcasted_iota(jnp.int32, (LANES, SSM_INNER), 1)
    expand_b = (e_col // SSM_HEADDIM == e_row).astype(F32).astype(BF16)
    even_half = col < SSM_HEADDIM
    a_neg = -jnp.exp(alog_ref[...])

    for r in range(tl // SSD_Q):
        r0 = r * SSD_Q
        rows = pl.ds(r0, SSD_Q)
        finish_in_proj(r0 // IN_ROWS)

        xbc_parts = []
        for c0 in range(0, SSM_CONV_DIM, D_MODEL):
            cols = slice(c0, c0 + D_MODEL)
            ext_ref = ext_refs[c0 // CONV_BLOCK]
            slab = ext_ref[pl.ds(r0, CONV_TAIL + SSD_Q), :]
            conv = cw_ref[SSM_CONV - 1:SSM_CONV, cols] * slab
            for lag in range(1, SSM_CONV):
                k = SSM_CONV - 1 - lag
                conv = conv + pltpu.roll(cw_ref[k:k + 1, cols] * slab, lag, axis=0)
            xbc_parts.append(_silu_of_twice(conv[CONV_TAIL:, :] + cb_ref[:, cols]))
        xs = jnp.concatenate(xbc_parts[0:2], axis=1)
        bm = xbc_parts[2].astype(BF16)
        bm_t = [xbc_parts[2][:, g * SSM_STATE:(g + 1) * SSM_STATE].T.astype(BF16)
                for g in range(SSM_GROUPS)]
        cm = xbc_parts[3]
        cm_b = cm.astype(BF16)
        xs_b = xs.astype(BF16)

        if r == 0:
            in_proj("dt")
            in_proj("uv")
        else:
            next_in_proj()
        dt = jax.nn.softplus(load(rows, "dt") + dtb_ref[...])
        acs = _dot_exact_rhs(tri_b, dt * a_neg)
        acs_t = acs.T
        dt_t = dt.T
        ea = jnp.exp(acs)
        last = acs[SSD_Q - 1:SSD_Q, :]
        wgt = dt * jnp.exp(last - acs)
        cdec = jnp.broadcast_to(jnp.exp(last), (16, LANES))
        expd = _dot_exact_lhs(jnp.concatenate([wgt, cdec], axis=0), expand_b)
        xw = (xs * expd[0:SSD_Q, :]).astype(BF16)
        cdec_x = expd[SSD_Q:SSD_Q + 1, :]

        u = _gelu_erf(load(rows, "uv", 0, D_MODEL))
        v = _gelu_erf(load(rows, "uv", D_MODEL, D_MODEL))
        mu = jnp.mean(v, axis=-1, keepdims=True)
        vc = v - mu
        var = jnp.mean(vc * vc, axis=-1, keepdims=True)
        vln = (vc * lax.rsqrt(var + EPS) * lng_ref[...] + lnb_ref[...]).astype(BF16)
        f_parts = []
        for g in range(SGU_GROUPS):
            wm = (ws_ref[g] * sgu_mask).astype(BF16)
            fg = _dot(wm, vln[:, g * LANES:(g + 1) * LANES]) + bst_ref[:, g:g + 1]
            f_parts.append(fg)
        ya = (u * jnp.concatenate(f_parts, axis=1)).astype(BF16)

        next_in_proj()
        if r == 0:
            in_proj("z")
        y_parts = []
        for g in range(SSM_GROUPS):
            if r == 0 and g == SSM_GROUPS // 2:
                in_proj("gates")
            if g in (1, 3, 6):
                next_in_proj()
            gs = slice(g * SSM_STATE, (g + 1) * SSM_STATE)
            cb = lax.dot_general(cm_b[:, gs], bm[:, gs], (((1,), (1,)), ((), ())),
                                 preferred_element_type=F32)
            st = state_ref[g]
            st_b = st.astype(BF16)
            cm_g = cm[:, gs]
            for j in range(SSM_HPG // 2):
                c0 = g * GROUP_W + j * LANES
                rhs = jnp.concatenate(
                    [xs_b[:, c0:c0 + LANES], st_b[:, j * LANES:(j + 1) * LANES]], axis=0)
                y_heads = []
                for k in (2 * j, 2 * j + 1):
                    h = g * SSM_HPG + k
                    sg = acs[:, h:h + 1] - acs_t[h:h + 1, :]
                    decay = jnp.exp(jnp.where(causal, sg, -jnp.inf))
                    scores = (cb * decay * dt_t[h:h + 1, :]).astype(BF16)
                    readout = (cm_g * ea[:, h:h + 1]).astype(BF16)
                    y_heads.append(_dot(jnp.concatenate([scores, readout], axis=1), rhs))
                y_parts.append(jnp.where(even_half, y_heads[0], y_heads[1]))
            new = _dot(bm_t[g], xw[:, g * GROUP_W:(g + 1) * GROUP_W])
            state_ref[g] = st * cdec_x[:, g * GROUP_W:(g + 1) * GROUP_W] + new
        y = jnp.concatenate(y_parts, axis=1) + xs * dskip_ref[...]

        yg = y * _silu(load(rows, "z"))
        n_parts = []
        for g in range(SSM_GROUPS):
            blk = yg[:, g * GROUP_W:(g + 1) * GROUP_W]
            ms = jnp.mean(blk * blk, axis=-1, keepdims=True)
            n_parts.append(blk * lax.rsqrt(ms + EPS))
        yb = (jnp.concatenate(n_parts, axis=1) * nrm_ref[...]).astype(BF16)

        merged = (jax.nn.sigmoid(load(rows, "gates", 0, D_MODEL)) * _dot(ya, wa_ref[...])
                  + jax.nn.sigmoid(load(rows, "gates", D_MODEL, D_MODEL)) * _dot(yb, wb_ref[...]))
        out_ref[rows, :] = merged.astype(BF16)

    for ext_ref in ext_refs:
        ext_ref[0:CONV_TAIL, :] = ext_ref[tl:tl + CONV_TAIL, :]


def _mixer_call(n3, w_main, w_tail, lng, lnb, w_s, b_st, conv_w, conv_b, dtb, alog, dskip, nrm, w_a, w_b):
    b, l, _ = n3.shape
    tl = min(MIX_TL, l)
    assert tl % IN_ROWS == 0 and IN_ROWS % SSD_Q == 0
    tile = lambda bi, i: (bi, i, 0)
    single = pl.Buffered(1)

    def resident(shape):
        return pl.BlockSpec(shape, lambda bi, i: (0,) * len(shape), pipeline_mode=single)

    return pl.pallas_call(
        _mixer_kernel,
        grid=(b, l // tl),
        in_specs=[
            pl.BlockSpec((None, tl, D_MODEL), tile),
            pl.BlockSpec((D_MODEL, MAIN_W), lambda bi, i: (0, 0), pipeline_mode=single),
            resident((D_MODEL, TAIL_W)),
            resident((1, D_MODEL)),
            resident((1, D_MODEL)),
            resident((SGU_GROUPS, SGU_BLOCK, SGU_BLOCK)),
            resident((SGU_BLOCK, SGU_GROUPS)),
            resident((SSM_CONV, SSM_CONV_DIM)),
            resident((1, SSM_CONV_DIM)),
            resident((1, LANES)),
            resident((1, LANES)),
            resident((1, SSM_INNER)),
            resident((1, SSM_INNER)),
            resident((D_MODEL, D_MODEL)),
            resident((SSM_INNER, D_MODEL)),
        ],
        out_specs=pl.BlockSpec((None, tl, D_MODEL), tile),
        out_shape=jax.ShapeDtypeStruct((b, l, D_MODEL), BF16),
        scratch_shapes=[
            pltpu.VMEM((SSM_GROUPS, SSM_STATE, GROUP_W), F32),
            *[pltpu.VMEM((tl + CONV_TAIL, CONV_BLOCK), F32)
              for _ in range(N_CONV_BLOCKS)],
            *[pltpu.VMEM((tl, width), F32)
              for _, _, width in SEGMENTS.values()],
        ],
        compiler_params=pltpu.CompilerParams(
            dimension_semantics=("parallel", "arbitrary"), vmem_limit_bytes=VMEM_LIMIT_BYTES),
        name="mixer",
    )(n3, w_main, w_tail, lng, lnb, w_s, b_st, conv_w, conv_b, dtb, alog, dskip, nrm, w_a, w_b)


def _tail_w_in(w):
    dt0 = MAIN_W
    dt, gates = w[:, dt0:dt0 + SSM_HEADS], w[:, dt0 + SSM_HEADS:]
    pad = jnp.zeros((w.shape[0], LANES - SSM_HEADS), w.dtype)
    return jnp.concatenate([gates, dt, pad], axis=1)


def _pad_lanes(v):
    return jnp.concatenate([v, jnp.zeros((LANES - v.shape[0],), v.dtype)]).reshape(1, LANES)


def kernel(x, ffn1_norm, ffn1_w_in, ffn1_w_out, mix_norm, w_in, sgu_ln_g, sgu_ln_b, sgu_w_s, sgu_b_s,
           conv_w, conv_b, dt_bias, a_log, d_skip, ssm_norm, w_a, w_b, w_o, ffn2_norm, ffn2_w_in,
           ffn2_w_out, final_norm):
    b, l, d = x.shape
    assert ffn1_norm.shape[0] == 1, "single-layer block"
    assert d == D_MODEL and l % SSD_Q == 0 and (b * l) % FFN_TM == 0
    row = lambda v: v.reshape(1, -1).astype(F32)
    bf = lambda w: w.astype(BF16)
    h1, n2 = _ffn1_call(x.reshape(b * l, d), row(ffn1_norm[0]), bf(ffn1_w_in[0]), bf(ffn1_w_out[0]),
                        row(mix_norm[0]))
    w_in_b = bf(w_in[0])
    merged = _mixer_call(
        n2.reshape(b, l, d), w_in_b, _tail_w_in(w_in_b),
        row(sgu_ln_g[0]), row(sgu_ln_b[0]), sgu_w_s[0], sgu_b_s[0].T,
        0.5 * conv_w[0], row(0.5 * conv_b[0]), _pad_lanes(dt_bias[0]), _pad_lanes(a_log[0]),
        row(jnp.repeat(d_skip[0], SSM_HEADDIM)), row(ssm_norm[0]), bf(w_a[0]), bf(w_b[0]))
    out = _ffn2_call(h1, merged.reshape(b * l, d), bf(w_o[0]), row(ffn2_norm[0]), bf(ffn2_w_in[0]),
                     bf(ffn2_w_out[0]), row(final_norm))
    return out.reshape(b, l, d)
```
